```python
import math
import jax, jax.numpy as jnp
from jax import lax
import numpy as np

D_MODEL = 1024
BATCH = 8
SEQ = 2048
DEPTH = 2
DEC_BATCH = 128
DEC_SEQ = 8
PAST_LEN = 16384
PAGE_SIZE = 128

N_AB_LAYERS = (DEPTH + 1) // 2
N_C_LAYERS = DEPTH // 2
A_HEADS = 16
A_HEAD_DIM = 64
A_D_INNER = A_HEADS * A_HEAD_DIM
A_GROUPS = 4
A_STATE = 128
A_CONV = 4
A_CONV_DIM = A_D_INNER + 2 * A_GROUPS * A_STATE
A_CHUNK = 128
A_PROJ = A_D_INNER + A_CONV_DIM + A_HEADS
B_CH = 1024
B_GROUP_CH = 16
B_GROUPS = B_CH // B_GROUP_CH
B_STATE = 64
IN_COLS = A_PROJ + B_CH
MIX_WIDTH = A_D_INNER + B_CH
C_CH = D_MODEL
C_WIDTH = 31
D_FF = 4 * D_MODEL
EPS = 1e-6

kernel_name = "hybrid_ssd_s5_conformer_decode_step"


def rms_norm(x, w):
    xf = x.astype(jnp.float32)
    y = xf * lax.rsqrt(jnp.mean(xf * xf, axis=-1, keepdims=True) + EPS)
    return (y * w.astype(jnp.float32)).astype(x.dtype)


def layer_norm(x, w, b):
    xf = x.astype(jnp.float32)
    mu = jnp.mean(xf, axis=-1, keepdims=True)
    xc = xf - mu
    y = xc * lax.rsqrt(jnp.mean(xc * xc, axis=-1, keepdims=True) + EPS)
    return (y * w.astype(jnp.float32) + b.astype(jnp.float32)).astype(x.dtype)


def causal_depthwise_conv(x, buf, w, b):
    width = w.shape[0]
    xp = jnp.concatenate([buf.astype(x.dtype), x], axis=1)
    y = lax.conv_general_dilated(xp, w.astype(x.dtype)[:, None, :], window_strides=(1,), padding="VALID",
                                 dimension_numbers=("NWC", "WIO", "NWC"), feature_group_count=x.shape[-1])
    return y + b.astype(x.dtype), xp[:, xp.shape[1] - (width - 1):]


def ssd_chunked_scan(xh, dt, a, bm, cm, h0):
    f32 = jnp.float32
    bsz, seqlen = xh.shape[0], xh.shape[1]
    q = math.gcd(seqlen, A_CHUNK)
    nc = seqlen // q
    hpg = A_HEADS // A_GROUPS
    xf = xh.astype(f32).reshape(bsz, nc, q, A_GROUPS, hpg, A_HEAD_DIM)
    dtc = dt.reshape(bsz, nc, q, A_GROUPS, hpg)
    bc = bm.astype(f32).reshape(bsz, nc, q, A_GROUPS, A_STATE)
    cc = cm.astype(f32).reshape(bsz, nc, q, A_GROUPS, A_STATE)
    cum = jnp.cumsum(dtc * a.reshape(A_GROUPS, hpg), axis=2)
    causal = jnp.tril(jnp.ones((q, q), dtype=bool))[:, :, None, None]
    seg = cum[:, :, :, None] - cum[:, :, None, :]
    decay = jnp.exp(jnp.where(causal, seg, -jnp.inf))
    scores = jnp.einsum("bcign,bcjgn->bcijg", cc, bc)
    y_diag = jnp.einsum("bcijge,bcjgep->bcigep", scores[..., None] * decay * dtc[:, :, None], xf)
    decay_end = jnp.exp(cum[:, :, -1:] - cum)
    chunk_states = jnp.einsum("bcjgn,bcjge,bcjgep->bcgepn", bc, decay_end * dtc, xf)
    chunk_decay = jnp.exp(cum[:, :, -1])

    def carry_step(h, inp):
        dec, st = inp
        return h * dec[..., None, None] + st, h

    h_start = h0.astype(f32).reshape(bsz, A_GROUPS, hpg, A_HEAD_DIM, A_STATE)
    h_final, h_in = lax.scan(carry_step, h_start,
                             (jnp.moveaxis(chunk_decay, 1, 0), jnp.moveaxis(chunk_states, 1, 0)))
    h_in = jnp.moveaxis(h_in, 0, 1)
    y_off = jnp.einsum("bcign,bcgepn,bcige->bcigep", cc, h_in, jnp.exp(cum))
    y = (y_diag + y_off).reshape(bsz, seqlen, A_HEADS, A_HEAD_DIM)
    return y, h_final.reshape(bsz, A_HEADS, A_HEAD_DIM, A_STATE)


def mamba2_mixer(proj, conv_buf, ssm_state, conv_w, conv_b, dt_bias, a_log, d_skip, norm_w):
    f32 = jnp.float32
    bsz, seqlen = proj.shape[0], proj.shape[1]
    z = proj[..., :A_D_INNER]
    xbc = proj[..., A_D_INNER:A_D_INNER + A_CONV_DIM]
    dt_raw = proj[..., A_D_INNER + A_CONV_DIM:]
    xbc, new_buf = causal_depthwise_conv(xbc, conv_buf, conv_w, conv_b)
    xbc = jax.nn.silu(xbc)
    xh = xbc[..., :A_D_INNER].reshape(bsz, seqlen, A_HEADS, A_HEAD_DIM)
    bm = xbc[..., A_D_INNER:A_D_INNER + A_GROUPS * A_STATE].reshape(bsz, seqlen, A_GROUPS, A_STATE)
    cm = xbc[..., A_D_INNER + A_GROUPS * A_STATE:].reshape(bsz, seqlen, A_GROUPS, A_STATE)
    dt = jax.nn.softplus(dt_raw.astype(f32) + dt_bias.astype(f32))
    a = -jnp.exp(a_log.astype(f32))
    y, h_final = ssd_chunked_scan(xh, dt, a, bm, cm, ssm_state)
    y = y + d_skip.astype(f32)[:, None] * xh.astype(f32)
    g = (y.reshape(bsz, seqlen, A_D_INNER) * jax.nn.silu(z.astype(f32)))
    g = g.reshape(bsz, seqlen, A_GROUPS, A_D_INNER // A_GROUPS)
    g = g * lax.rsqrt(jnp.mean(g * g, axis=-1, keepdims=True) + EPS)
    y = g.reshape(bsz, seqlen, A_D_INNER) * norm_w.astype(f32)
    return y.astype(proj.dtype), new_buf, h_final


def complex_affine_combine(left, right):
    ar_l, ai_l, br_l, bi_l = left
    ar_r, ai_r, br_r, bi_r = right
    ar = ar_r * ar_l - ai_r * ai_l
    ai = ar_r * ai_l + ai_r * ar_l
    br = ar_r * br_l - ai_r * bi_l + br_r
    bi = ar_r * bi_l + ai_r * br_l + bi_r
    return ar, ai, br, bi


def s5_mixer(u, h0_re, h0_im, lam_re, lam_im, log_step, b_re, b_im, c_re, c_im, d_skip, w_glu, b_glu):
    f32 = jnp.float32
    bsz, seqlen = u.shape[0], u.shape[1]
    uf = u.astype(f32).reshape(bsz, seqlen, B_GROUPS, B_GROUP_CH)
    lr, li = lam_re.astype(f32), lam_im.astype(f32)
    step = jnp.exp(log_step.astype(f32))[:, None]
    mag = jnp.exp(lr * step)
    ab_re, ab_im = mag * jnp.cos(li * step), mag * jnp.sin(li * step)
    den = lr * lr + li * li
    k_re = ((ab_re - 1.0) * lr + ab_im * li) / den
    k_im = (ab_im * lr - (ab_re - 1.0) * li) / den
    br, bi = b_re.astype(f32), b_im.astype(f32)
    bb_re = k_re[..., None] * br - k_im[..., None] * bi
    bb_im = k_re[..., None] * bi + k_im[..., None] * br
    bu_re = jnp.einsum("gpc,blgc->blgp", bb_re, uf)
    bu_im = jnp.einsum("gpc,blgc->blgp", bb_im, uf)
    a_re = jnp.broadcast_to(ab_re, (1, seqlen, B_GROUPS, B_STATE))
    a_im = jnp.broadcast_to(ab_im, (1, seqlen, B_GROUPS, B_STATE))
    p_re, p_im, h_re, h_im = lax.associative_scan(complex_affine_combine, (a_re, a_im, bu_re, bu_im), axis=1)
    s_re, s_im = h0_re.astype(f32)[:, None], h0_im.astype(f32)[:, None]
    h_re = h_re + p_re * s_re - p_im * s_im
    h_im = h_im + p_re * s_im + p_im * s_re
    y = (jnp.einsum("gcp,blgp->blgc", c_re.astype(f32), h_re)
         - jnp.einsum("gcp,blgp->blgc", c_im.astype(f32), h_im)
         + d_skip.astype(f32) * uf)
    y = jax.nn.gelu(y.reshape(bsz, seqlen, B_CH)).astype(u.dtype)
    out = y * jax.nn.sigmoid(y @ w_glu + b_glu)
    return out, h_re[:, -1], h_im[:, -1]


def conformer_conv_module(xn, buf, w_pw1, b_pw1, w_dw, b_dw, ln_w, ln_b, w_pw2, b_pw2):
    h = xn @ w_pw1 + b_pw1
    h = h[..., :C_CH] * jax.nn.sigmoid(h[..., C_CH:])
    h, new_buf = causal_depthwise_conv(h, buf, w_dw, b_dw)
    h = jax.nn.silu(layer_norm(h, ln_w, ln_b))
    return h @ w_pw2 + b_pw2, new_buf


def run_trunk(x, a_conv_state, a_ssm_state, b_state_re, b_state_im, c_conv_state, p):
    new_a_conv, new_a_ssm, new_b_re, new_b_im, new_c_conv = [], [], [], [], []
    ab_i = 0
    c_i = 0
    for layer in range(DEPTH):
        xn = rms_norm(x, p["norm_mix"][layer])
        if layer % 2 == 0:
            proj = xn @ p["w_in_ab"][ab_i]
            ya, buf_a, h_a = mamba2_mixer(proj[..., :A_PROJ], a_conv_state[ab_i], a_ssm_state[ab_i],
                                          p["a_conv_w"][ab_i], p["a_conv_b"][ab_i], p["a_dt_bias"][ab_i],
                                          p["a_log"][ab_i], p["a_d"][ab_i], p["a_norm"][ab_i])
            yb, hb_re, hb_im = s5_mixer(proj[..., A_PROJ:], b_state_re[ab_i], b_state_im[ab_i],
                                        p["s5_lam_re"][ab_i], p["s5_lam_im"][ab_i], p["s5_log_step"][ab_i],
                                        p["s5_b_re"][ab_i], p["s5_b_im"][ab_i], p["s5_c_re"][ab_i],
                                        p["s5_c_im"][ab_i], p["s5_d"][ab_i], p["s5_w_glu"][ab_i],
                                        p["s5_b_glu"][ab_i])
            x = x + jnp.concatenate([ya, yb], axis=-1) @ p["w_out_ab"][ab_i]
            new_a_conv.append(buf_a)
            new_a_ssm.append(h_a)
            new_b_re.append(hb_re)
            new_b_im.append(hb_im)
            ab_i += 1
        else:
            yc, buf_c = conformer_conv_module(xn, c_conv_state[c_i], p["c_w_pw1"][c_i], p["c_b_pw1"][c_i],
                                              p["c_w_dw"][c_i], p["c_b_dw"][c_i], p["c_ln_w"][c_i],
                                              p["c_ln_b"][c_i], p["c_w_pw2"][c_i], p["c_b_pw2"][c_i])
            x = x + yc
            new_c_conv.append(buf_c)
            c_i += 1
        xn = rms_norm(x, p["norm_ff"][layer])
        x = x + jnp.square(jax.nn.relu(xn @ p["w_ff1"][layer])) @ p["w_ff2"][layer]
    y = rms_norm(x, p["norm_final"])
    return (y, jnp.stack(new_a_conv), jnp.stack(new_a_ssm), jnp.stack(new_b_re), jnp.stack(new_b_im),
            jnp.stack(new_c_conv))


def setup_inputs(seed: int = 0) -> dict:
    key = jax.random.key(seed)
    ks = iter(jax.random.split(key, 64))
    f32 = jnp.float32

    def nrm(shape, scale):
        return scale * jax.random.normal(next(ks), shape, f32)

    def unif(shape, lo, hi):
        return jax.random.uniform(next(ks), shape, f32, lo, hi)

    n_ab, n_c = N_AB_LAYERS, N_C_LAYERS
    x_prompt = nrm((BATCH, SEQ, D_MODEL), 1.0)
    x_sample = nrm((DEC_BATCH, DEC_SEQ, D_MODEL), 1.0)
    state_a_conv = nrm((n_ab, DEC_BATCH, A_CONV - 1, A_CONV_DIM), 1.0)
    state_a_ssm = nrm((n_ab, DEC_BATCH, A_HEADS, A_HEAD_DIM, A_STATE), 0.5)
    state_b_re = nrm((n_ab, DEC_BATCH, B_GROUPS, B_STATE), 0.5)
    state_b_im = nrm((n_ab, DEC_BATCH, B_GROUPS, B_STATE), 0.5)
    state_c_conv = nrm((n_c, DEC_BATCH, C_WIDTH - 1, C_CH), 1.0)
    norm_mix = 1.0 + nrm((DEPTH, D_MODEL), 0.02)
    norm_ff = 1.0 + nrm((DEPTH, D_MODEL), 0.02)
    norm_final = 1.0 + nrm((D_MODEL,), 0.02)
    w_in_ab = nrm((n_ab, D_MODEL, IN_COLS), D_MODEL ** -0.5)
    a_conv_w = nrm((n_ab, A_CONV, A_CONV_DIM), A_CONV ** -0.5)
    a_conv_b = nrm((n_ab, A_CONV_DIM), 0.02)
    dt0 = jnp.exp(unif((n_ab, A_HEADS), math.log(1e-3), math.log(1e-1)))
    a_dt_bias = dt0 + jnp.log(-jnp.expm1(-dt0))
    a_log = jnp.log(unif((n_ab, A_HEADS), 1.0, 16.0))
    a_d = 1.0 + nrm((n_ab, A_HEADS), 0.1)
    a_norm = 1.0 + nrm((n_ab, A_D_INNER), 0.02)
    s5_lam_re = -0.5 + nrm((n_ab, B_GROUPS, B_STATE), 0.01)
    s5_lam_im = math.pi * jnp.arange(B_STATE, dtype=f32) + nrm((n_ab, B_GROUPS, B_STATE), 0.01)
    s5_log_step = unif((n_ab, B_GROUPS), math.log(1e-3), math.log(1e-1))
    s5_b_re = nrm((n_ab, B_GROUPS, B_STATE, B_GROUP_CH), (2 * B_GROUP_CH) ** -0.5)
    s5_b_im = nrm((n_ab, B_GROUPS, B_STATE, B_GROUP_CH), (2 * B_GROUP_CH) ** -0.5)
    s5_c_re = nrm((n_ab, B_GROUPS, B_GROUP_CH, B_STATE), (2 * B_STATE) ** -0.5)
    s5_c_im = nrm((n_ab, B_GROUPS, B_GROUP_CH, B_STATE), (2 * B_STATE) ** -0.5)
    s5_d = nrm((n_ab, B_GROUPS, B_GROUP_CH), 0.5)
    s5_w_glu = nrm((n_ab, B_CH, B_CH), B_CH ** -0.5)
    s5_b_glu = nrm((n_ab, B_CH), 0.02)
    w_out_ab = nrm((n_ab, MIX_WIDTH, D_MODEL), MIX_WIDTH ** -0.5)
    c_w_pw1 = nrm((n_c, D_MODEL, 2 * C_CH), D_MODEL ** -0.5)
    c_b_pw1 = nrm((n_c, 2 * C_CH), 0.02)
    c_w_dw = nrm((n_c, C_WIDTH, C_CH), C_WIDTH ** -0.5)
    c_b_dw = nrm((n_c, C_CH), 0.02)
    c_ln_w = 1.0 + nrm((n_c, C_CH), 0.02)
    c_ln_b = nrm((n_c, C_CH), 0.02)
    c_w_pw2 = nrm((n_c, C_CH, D_MODEL), C_CH ** -0.5)
    c_b_pw2 = nrm((n_c, D_MODEL), 0.02)
    w_ff1 = nrm((DEPTH, D_MODEL, D_FF), D_MODEL ** -0.5)
    w_ff2 = nrm((DEPTH, D_FF, D_MODEL), D_FF ** -0.5)
    return {"x_prompt": x_prompt, "x_sample": x_sample,
            "state_a_conv": state_a_conv, "state_a_ssm": state_a_ssm,
            "state_b_re": state_b_re, "state_b_im": state_b_im, "state_c_conv": state_c_conv,
            "norm_mix": norm_mix, "norm_ff": norm_ff, "norm_final": norm_final,
            "w_in_ab": w_in_ab, "a_conv_w": a_conv_w, "a_conv_b": a_conv_b, "a_dt_bias": a_dt_bias,
            "a_log": a_log, "a_d": a_d, "a_norm": a_norm,
            "s5_lam_re": s5_lam_re, "s5_lam_im": s5_lam_im, "s5_log_step": s5_log_step,
            "s5_b_re": s5_b_re, "s5_b_im": s5_b_im, "s5_c_re": s5_c_re, "s5_c_im": s5_c_im,
            "s5_d": s5_d, "s5_w_glu": s5_w_glu, "s5_b_glu": s5_b_glu, "w_out_ab": w_out_ab,
            "c_w_pw1": c_w_pw1, "c_b_pw1": c_b_pw1, "c_w_dw": c_w_dw, "c_b_dw": c_b_dw,
            "c_ln_w": c_ln_w, "c_ln_b": c_ln_b, "c_w_pw2": c_w_pw2, "c_b_pw2": c_b_pw2,
            "w_ff1": w_ff1, "w_ff2": w_ff2}


def reference(x_prompt, x_sample, state_a_conv, state_a_ssm, state_b_re, state_b_im, state_c_conv,
              norm_mix, norm_ff, norm_final, w_in_ab, a_conv_w, a_conv_b, a_dt_bias, a_log, a_d, a_norm,
              s5_lam_re, s5_lam_im, s5_log_step, s5_b_re, s5_b_im, s5_c_re, s5_c_im, s5_d, s5_w_glu,
              s5_b_glu, w_out_ab, c_w_pw1, c_b_pw1, c_w_dw, c_b_dw, c_ln_w, c_ln_b, c_w_pw2, c_b_pw2,
              w_ff1, w_ff2):
    p = dict(norm_mix=norm_mix, norm_ff=norm_ff, norm_final=norm_final, w_in_ab=w_in_ab,
             a_conv_w=a_conv_w, a_conv_b=a_conv_b, a_dt_bias=a_dt_bias, a_log=a_log, a_d=a_d, a_norm=a_norm,
             s5_lam_re=s5_lam_re, s5_lam_im=s5_lam_im, s5_log_step=s5_log_step, s5_b_re=s5_b_re,
             s5_b_im=s5_b_im, s5_c_re=s5_c_re, s5_c_im=s5_c_im, s5_d=s5_d, s5_w_glu=s5_w_glu,
             s5_b_glu=s5_b_glu, w_out_ab=w_out_ab, c_w_pw1=c_w_pw1, c_b_pw1=c_b_pw1, c_w_dw=c_w_dw,
             c_b_dw=c_b_dw, c_ln_w=c_ln_w, c_ln_b=c_ln_b, c_w_pw2=c_w_pw2, c_b_pw2=c_b_pw2,
             w_ff1=w_ff1, w_ff2=w_ff2)
    nb = x_prompt.shape[0]
    zero_a_conv = jnp.zeros((N_AB_LAYERS, nb, A_CONV - 1, A_CONV_DIM), x_prompt.dtype)
    zero_a_ssm = jnp.zeros((N_AB_LAYERS, nb, A_HEADS, A_HEAD_DIM, A_STATE), jnp.float32)
    zero_b = jnp.zeros((N_AB_LAYERS, nb, B_GROUPS, B_STATE), jnp.float32)
    zero_c_conv = jnp.zeros((N_C_LAYERS, nb, C_WIDTH - 1, C_CH), x_prompt.dtype)
    y_prompt, p_a_conv, p_a_ssm, p_b_re, p_b_im, p_c_conv = run_trunk(
        x_prompt, zero_a_conv, zero_a_ssm, zero_b, zero_b, zero_c_conv, p)
    y_sample, s_a_conv, s_a_ssm, s_b_re, s_b_im, s_c_conv = run_trunk(
        x_sample, state_a_conv, state_a_ssm, state_b_re, state_b_im, state_c_conv, p)
    return (y_prompt, y_sample, p_a_conv, p_a_ssm, p_b_re, p_b_im, p_c_conv,
            s_a_conv, s_a_ssm, s_b_re, s_b_im, s_c_conv)
```

```python
import functools
import math

import jax
import jax.numpy as jnp
from jax import lax
from jax.experimental import pallas as pl
from jax.experimental.pallas import tpu as pltpu

F32 = jnp.float32
BF16 = jnp.bfloat16

D_MODEL = 1024
A_HEADS = 16
A_HEAD_DIM = 64
A_D_INNER = A_HEADS * A_HEAD_DIM
A_GROUPS = 4
A_STATE = 128
A_CONV = 4
A_CONV_DIM = A_D_INNER + 2 * A_GROUPS * A_STATE
A_CHUNK = 128
B_CH = 1024
B_GROUP_CH = 16
B_GROUPS = B_CH // B_GROUP_CH
B_STATE = 64
B_NSTATE = B_GROUPS * B_STATE
C_CH = D_MODEL
C_WIDTH = 31
D_FF = 4 * D_MODEL
EPS = 1e-6

SUBLANES = 8
LANES = 128
VMEM_LIMIT_BYTES = 56 * 1024 * 1024

COL_Z = 0
COL_X = A_D_INNER
COL_BC = 2 * A_D_INNER
COL_U = 3 * A_D_INNER
COL_DT = 4 * A_D_INNER
PROJ_COLS = COL_DT + LANES

NEG_BIG = -1e30
N_SLAB = B_CH // LANES
S5_SLABS = B_NSTATE // LANES
S5_BLK = 8
CONV_HIST = 32


def _params(n_grid):
    return pltpu.CompilerParams(dimension_semantics=("arbitrary",) * n_grid,
                                vmem_limit_bytes=VMEM_LIMIT_BYTES)


def _const_spec(shape):
    nd = len(shape)
    return pl.BlockSpec(shape, lambda *_: (0,) * nd, pipeline_mode=pl.Buffered(1))


def _rms(x, w):
    return x * lax.rsqrt(jnp.mean(x * x, axis=-1, keepdims=True) + EPS) * w


def _dot(a, b):
    return jnp.dot(a, b, preferred_element_type=F32)


def _dot_nt(a, b):
    return lax.dot_general(a, b, (((1,), (1,)), ((), ())), preferred_element_type=F32)


def _dot_tn(a, b):
    return lax.dot_general(a, b, (((0,), (0,)), ((), ())), preferred_element_type=F32)


def _split_bf16(v):
    hi = v.astype(BF16)
    lo = (v - hi.astype(F32)).astype(BF16)
    return hi, lo


def _inproj_kernel(x_ref, nw_ref, w_ref, o_ref):
    xn = _rms(x_ref[...], nw_ref[...]).astype(BF16)
    n = o_ref.shape[1]
    step = 512
    for c0 in range(0, n, step):
        c1 = min(c0 + step, n)
        o_ref[:, c0:c1] = _dot(xn, w_ref[:, c0:c1])


def _inproj(x2d, nw, w):
    m = x2d.shape[0]
    tm = 512
    return pl.pallas_call(
        _inproj_kernel,
        out_shape=jax.ShapeDtypeStruct((m, PROJ_COLS), F32),
        grid=(m // tm,),
        in_specs=[pl.BlockSpec((tm, D_MODEL), lambda i: (i, 0)),
                  _const_spec((1, D_MODEL)),
                  _const_spec((D_MODEL, PROJ_COLS))],
        out_specs=pl.BlockSpec((tm, PROJ_COLS), lambda i: (i, 0)),
        compiler_params=_params(1),
        name="inproj",
    )(x2d, nw, w)


def _ssd_tile(z_ref, xs_ref, bc_ref, dt_ref, cw_ref, cb_ref, dtb_ref, alog_ref, dsk_ref, nw_ref, eh_ref, eht_ref,
              win_ref, h_in_ref, h_out_ref, ya_ref, *, n_seq, q):
    tm = n_seq * q
    raw = jnp.concatenate([xs_ref[...], bc_ref[...]], axis=1)
    for s in range(n_seq):
        win_ref[s, SUBLANES:SUBLANES + q, :] = raw[s * q:(s + 1) * q]
    conv = jnp.zeros((tm, A_CONV_DIM), F32)
    for k in range(A_CONV):
        off = SUBLANES - (A_CONV - 1) + k
        tap = jnp.concatenate([win_ref[s, off:off + q, :] for s in range(n_seq)], axis=0)
        conv = conv + tap * cw_ref[k:k + 1, :]
    xc = conv + cb_ref[...]
    xc = xc * jax.nn.sigmoid(xc)
    x = xc[:, :A_D_INNER]
    bm = xc[:, A_D_INNER:A_D_INNER + A_GROUPS * A_STATE].astype(BF16)
    cm = xc[:, A_D_INNER + A_GROUPS * A_STATE:].astype(BF16)
    xb = x.astype(BF16)

    dt = jax.nn.softplus(dt_ref[...] + dtb_ref[...])
    a = -jnp.exp(alog_ref[...])
    da = dt * a
    ii = lax.broadcasted_iota(jnp.int32, (tm, tm), 0)
    jj = lax.broadcasted_iota(jnp.int32, (tm, tm), 1)
    causal = jj <= ii
    if n_seq > 1:
        causal = causal & ((ii // q) == (jj // q))
    tri = jnp.where(causal, 1.0, 0.0).astype(F32)
    cum = jnp.dot(tri, da, precision=lax.Precision.HIGHEST, preferred_element_type=F32)
    cum_t = cum.T
    dt_t = dt.T
    last = jnp.concatenate(
        [jnp.broadcast_to(cum[s * q + q - 1:s * q + q, :], (q, LANES)) for s in range(n_seq)], axis=0)

    eh = eh_ref[...]

    def expand(v):
        hi, lo = _split_bf16(v)
        return _dot(hi, eh) + _dot(lo, eh)

    exp_cum = expand(jnp.exp(cum))
    w_state = expand(jnp.exp(last - cum) * dt)
    xw = (x * w_state).astype(BF16)
    lane = lax.broadcasted_iota(jnp.int32, (tm, LANES), 1)
    low_half = lane < A_HEAD_DIM

    decs = []
    for s in range(n_seq):
        col = s * q + q - 1
        lb = jnp.broadcast_to(jnp.exp(cum_t[:, col:col + 1]), (LANES, LANES))
        hi, lo = _split_bf16(lb)
        decs.append(_dot(eht_ref[...], hi) + _dot(eht_ref[...], lo))

    heads_per_group = A_HEADS // A_GROUPS
    gw = heads_per_group * A_HEAD_DIM
    y_parts = []
    for g in range(A_GROUPS):
        cg = cm[:, g * A_STATE:(g + 1) * A_STATE]
        bg = bm[:, g * A_STATE:(g + 1) * A_STATE]
        scores = _dot_nt(cg, bg)
        pair_out = []
        for pr in range(heads_per_group // 2):
            lane0 = g * gw + pr * LANES
            xp = xb[:, lane0:lane0 + LANES]
            acc = jnp.zeros((tm, LANES), F32)
            for half in range(2):
                e = g * heads_per_group + pr * 2 + half
                seg = cum[:, e:e + 1] - cum_t[e:e + 1, :]
                dec = jnp.exp(jnp.where(causal, seg, NEG_BIG))
                m = (scores * dec * dt_t[e:e + 1, :]).astype(BF16)
                xh = jnp.where(low_half if half == 0 else jnp.logical_not(low_half), xp, jnp.zeros_like(xp))
                acc = acc + _dot(m, xh)
            pair_out.append(acc)
        y_diag = jnp.concatenate(pair_out, axis=1)
        offs = []
        for s in range(n_seq):
            r0, r1 = s * q, (s + 1) * q
            h_old = h_in_ref[s, g * gw:(g + 1) * gw, :]
            offs.append(_dot_nt(cg[r0:r1], h_old.astype(BF16)))
            st = _dot_tn(xw[r0:r1, g * gw:(g + 1) * gw], bg[r0:r1])
            h_out_ref[s, g * gw:(g + 1) * gw, :] = h_old * decs[s][g * gw:(g + 1) * gw, :] + st
        y_off = jnp.concatenate(offs, axis=0) * exp_cum[:, g * gw:(g + 1) * gw]
        y_parts.append(y_diag + y_off)
    y = jnp.concatenate(y_parts, axis=1) + dsk_ref[...] * x

    z = z_ref[...]
    gt = y * (z * jax.nn.sigmoid(z))
    outs = []
    for g in range(A_GROUPS):
        gg = gt[:, g * gw:(g + 1) * gw]
        outs.append(gg * lax.rsqrt(jnp.mean(gg * gg, axis=-1, keepdims=True) + EPS))
    ya_ref[...] = jnp.concatenate(outs, axis=1) * nw_ref[...]


def _mamba_prompt_kernel(z_ref, xs_ref, bc_ref, dt_ref, cw_ref, cb_ref, dtb_ref, alog_ref, dsk_ref, nw_ref, eh_ref,
                         eht_ref, ya_ref, cout_ref, h_ref, win_ref):
    q = A_CHUNK

    @pl.when(pl.program_id(1) == 0)
    def _():
        win_ref[0, 0:SUBLANES, :] = jnp.zeros((SUBLANES, A_CONV_DIM), F32)
        h_ref[...] = jnp.zeros(h_ref.shape, F32)

    _ssd_tile(z_ref, xs_ref, bc_ref, dt_ref, cw_ref, cb_ref, dtb_ref, alog_ref, dsk_ref, nw_ref, eh_ref, eht_ref,
              win_ref, h_ref, h_ref, ya_ref, n_seq=1, q=q)
    cout_ref[0] = win_ref[0, SUBLANES + q - (A_CONV - 1):SUBLANES + q, :]
    win_ref[0, 0:SUBLANES, :] = win_ref[0, q:q + SUBLANES, :]


def _mamba_sample_kernel(z_ref, xs_ref, bc_ref, dt_ref, cin_ref, hin_ref, cw_ref, cb_ref, dtb_ref, alog_ref,
                         dsk_ref, nw_ref, eh_ref, eht_ref, ya_ref, cout_ref, h_ref, win_ref, *, n_seq, q):
    lo = SUBLANES - (A_CONV - 1)
    for s in range(n_seq):
        win_ref[s, lo:SUBLANES, :] = cin_ref[s]
    _ssd_tile(z_ref, xs_ref, bc_ref, dt_ref, cw_ref, cb_ref, dtb_ref, alog_ref, dsk_ref, nw_ref, eh_ref, eht_ref,
              win_ref, hin_ref, h_ref, ya_ref, n_seq=n_seq, q=q)
    for s in range(n_seq):
        cout_ref[s] = win_ref[s, SUBLANES + q - (A_CONV - 1):SUBLANES + q, :]


def _mamba_weights_specs():
    return [_const_spec((A_CONV, A_CONV_DIM)), _const_spec((1, A_CONV_DIM)), _const_spec((1, LANES)),
            _const_spec((1, LANES)), _const_spec((1, A_D_INNER)), _const_spec((1, A_D_INNER)),
            _const_spec((LANES, A_D_INNER)), _const_spec((A_D_INNER, LANES))]


def _mamba_prompt(proj, wts, nb, seqlen):
    q = A_CHUNK
    nc = seqlen // q
    wide = A_D_INNER // LANES

    def row(b, c):
        return b * nc + c

    return pl.pallas_call(
        _mamba_prompt_kernel,
        out_shape=[jax.ShapeDtypeStruct((nb * seqlen, A_D_INNER), F32),
                   jax.ShapeDtypeStruct((nb, A_CONV - 1, A_CONV_DIM), F32),
                   jax.ShapeDtypeStruct((nb, A_D_INNER, A_STATE), F32)],
        grid=(nb, nc),
        in_specs=[pl.BlockSpec((q, A_D_INNER), lambda b, c: (row(b, c), 0)),
                  pl.BlockSpec((q, A_D_INNER), lambda b, c: (row(b, c), 1)),
                  pl.BlockSpec((q, A_D_INNER), lambda b, c: (row(b, c), 2)),
                  pl.BlockSpec((q, LANES), lambda b, c: (row(b, c), 4 * wide))] + _mamba_weights_specs(),
        out_specs=[pl.BlockSpec((q, A_D_INNER), lambda b, c: (row(b, c), 0)),
                   pl.BlockSpec((1, A_CONV - 1, A_CONV_DIM), lambda b, c: (b, 0, 0)),
                   pl.BlockSpec((1, A_D_INNER, A_STATE), lambda b, c: (b, 0, 0))],
        scratch_shapes=[pltpu.VMEM((1, SUBLANES + q, A_CONV_DIM), F32)],
        compiler_params=_params(2),
        name="mamba_prompt",
    )(proj, proj, proj, proj, *wts)


def _mamba_sample(proj, conv_state, ssm_state, wts, nb, seqlen):
    n_seq = 8
    q = seqlen
    tm = n_seq * q
    wide = A_D_INNER // LANES
    kern = functools.partial(_mamba_sample_kernel, n_seq=n_seq, q=q)
    return pl.pallas_call(
        kern,
        out_shape=[jax.ShapeDtypeStruct((nb * seqlen, A_D_INNER), F32),
                   jax.ShapeDtypeStruct((nb, A_CONV - 1, A_CONV_DIM), F32),
                   jax.ShapeDtypeStruct((nb, A_D_INNER, A_STATE), F32)],
        grid=(nb // n_seq,),
        in_specs=[pl.BlockSpec((tm, A_D_INNER), lambda i: (i, 0)),
                  pl.BlockSpec((tm, A_D_INNER), lambda i: (i, 1)),
                  pl.BlockSpec((tm, A_D_INNER), lambda i: (i, 2)),
                  pl.BlockSpec((tm, LANES), lambda i: (i, 4 * wide)),
                  pl.BlockSpec((n_seq, A_CONV - 1, A_CONV_DIM), lambda i: (i, 0, 0)),
                  pl.BlockSpec((n_seq, A_D_INNER, A_STATE), lambda i: (i, 0, 0))] + _mamba_weights_specs(),
        out_specs=[pl.BlockSpec((tm, A_D_INNER), lambda i: (i, 0)),
                   pl.BlockSpec((n_seq, A_CONV - 1, A_CONV_DIM), lambda i: (i, 0, 0)),
                   pl.BlockSpec((n_seq, A_D_INNER, A_STATE), lambda i: (i, 0, 0))],
        scratch_shapes=[pltpu.VMEM((n_seq, SUBLANES + q, A_CONV_DIM), F32)],
        compiler_params=_params(1),
        name="mamba_sample",
    )(proj, proj, proj, proj, conv_state, ssm_state, *wts)


def _s5_param_kernel(lr_ref, li_ref, ls_ref, br_ref, bi_ref, are_ref, aim_ref, bbr_ref, bbi_ref):
    lr = lr_ref[...]
    li = li_ref[...]
    step = jnp.exp(ls_ref[...])
    mag = jnp.exp(lr * step)
    ab_re = mag * jnp.cos(li * step)
    ab_im = mag * jnp.sin(li * step)
    den = lr * lr + li * li
    k_re = ((ab_re - 1.0) * lr + ab_im * li) / den
    k_im = (ab_im * lr - (ab_re - 1.0) * li) / den
    br = br_ref[...]
    bi = bi_ref[...]
    are_ref[...] = ab_re
    aim_ref[...] = ab_im
    bbr_ref[...] = k_re * br - k_im * bi
    bbi_ref[...] = k_re * bi + k_im * br


def _s5_params(lam_re, lam_im, log_step, b_re, b_im):
    g, p, c = B_GROUPS, B_STATE, B_GROUP_CH
    return pl.pallas_call(
        _s5_param_kernel,
        out_shape=[jax.ShapeDtypeStruct((g, 1, p), F32), jax.ShapeDtypeStruct((g, 1, p), F32),
                   jax.ShapeDtypeStruct((g, c, p), F32), jax.ShapeDtypeStruct((g, c, p), F32)],
        name="s5_params",
    )(lam_re.reshape(g, 1, p), lam_im.reshape(g, 1, p), log_step.reshape(g, 1, 1),
      jnp.swapaxes(b_re, 1, 2), jnp.swapaxes(b_im, 1, 2))


def _block_diag(m, blk):
    g, r, c = m.shape
    eye = jnp.eye(blk, dtype=m.dtype)
    t = m.reshape(g // blk, blk, r, c)
    return (t[:, :, :, None, :] * eye[None, :, None, :, None]).reshape(g // blk, blk * r, blk * c)


def _gelu_tanh(x):
    return x * (0.5 * (1.0 + jnp.tanh(math.sqrt(2.0 / math.pi) * (x + 0.044715 * (x * x * x)))))


def _unit_row(sq, tc):
    u, b = divmod(sq, SUBLANES)
    return u * tc * SUBLANES + b


def _s5_kernel(*refs, n_unit, tc, has_state):
    if has_state:
        (u_ref, sre_ref, sim_ref, are_ref, aim_ref, bb_ref, cc_ref, dsk_ref, wg_ref, bg_ref,
         o_ref, hre_ref, him_ref, ubuf, buf, hst, obuf) = refs
    else:
        (u_ref, are_ref, aim_ref, bb_ref, cc_ref, dsk_ref, wg_ref, bg_ref,
         o_ref, hre_ref, him_ref, ubuf, buf, hst, obuf) = refs
    nb = SUBLANES * n_unit
    half = S5_SLABS

    for sq in range(nb):
        for i in range(N_SLAB):
            ubuf[i, pl.ds(_unit_row(sq, tc), tc, stride=SUBLANES), :] = u_ref[sq, :, i * LANES:(i + 1) * LANES]

    per = S5_SLABS // N_SLAB
    for i in range(N_SLAB):
        r = _dot(ubuf[i].astype(BF16), bb_ref[i])
        for k in range(per):
            buf[per * i + k] = r[:, k * LANES:(k + 1) * LANES]
            buf[half + per * i + k] = r[:, (per + k) * LANES:(per + k + 1) * LANES]

    if has_state:
        for j in range(S5_SLABS):
            hst[j] = sre_ref[:, j * LANES:(j + 1) * LANES]
            hst[half + j] = sim_ref[:, j * LANES:(j + 1) * LANES]
    else:
        @pl.when(pl.program_id(0) == 0)
        def _():
            hst[...] = jnp.zeros(hst.shape, F32)

    jb = 4

    def slab_group(jg, carry):
        j0 = jg * jb
        a_re = [jnp.broadcast_to(are_ref[pl.ds(j0 + k, 1), :], (SUBLANES, LANES)) for k in range(jb)]
        a_im = [jnp.broadcast_to(aim_ref[pl.ds(j0 + k, 1), :], (SUBLANES, LANES)) for k in range(jb)]
        for u in range(n_unit):
            rows = slice(u * SUBLANES, (u + 1) * SUBLANES)
            h0 = tuple(hst[j0 + k, rows, :] for k in range(jb)) + tuple(hst[half + j0 + k, rows, :] for k in range(jb))

            def step(t, hs, u=u):
                r0 = pl.multiple_of((u * tc + t) * SUBLANES, SUBLANES)
                new_re, new_im = [], []
                for k in range(jb):
                    h_re, h_im = hs[k], hs[jb + k]
                    b_re = buf[j0 + k, pl.ds(r0, SUBLANES), :]
                    b_im = buf[half + j0 + k, pl.ds(r0, SUBLANES), :]
                    n_re = a_re[k] * h_re - a_im[k] * h_im + b_re
                    n_im = a_re[k] * h_im + a_im[k] * h_re + b_im
                    buf[j0 + k, pl.ds(r0, SUBLANES), :] = n_re
                    buf[half + j0 + k, pl.ds(r0, SUBLANES), :] = n_im
                    new_re.append(n_re)
                    new_im.append(n_im)
                return tuple(new_re) + tuple(new_im)

            hs = lax.fori_loop(0, tc, step, h0, unroll=min(tc, 8))
            for k in range(jb):
                hst[j0 + k, rows, :] = hs[k]
                hst[half + j0 + k, rows, :] = hs[jb + k]
        return carry

    lax.fori_loop(0, S5_SLABS // jb, slab_group, 0)

    for j in range(S5_SLABS):
        hre_ref[:, j * LANES:(j + 1) * LANES] = hst[j]
        him_ref[:, j * LANES:(j + 1) * LANES] = hst[half + j]

    ys = []
    for i in range(N_SLAB):
        lhs = jnp.concatenate([buf[per * i + k] for k in range(per)]
                              + [buf[half + per * i + k] for k in range(per)], axis=1).astype(BF16)
        yi = _dot(lhs, cc_ref[i]) + dsk_ref[:, i * LANES:(i + 1) * LANES] * ubuf[i]
        ys.append(_gelu_tanh(yi))
    y = jnp.concatenate(ys, axis=1)
    out = y * jax.nn.sigmoid(_dot(y.astype(BF16), wg_ref[...]) + bg_ref[...])
    for i in range(N_SLAB):
        obuf[i] = out[:, i * LANES:(i + 1) * LANES]
    for sq in range(nb):
        for i in range(N_SLAB):
            o_ref[sq, :, i * LANES:(i + 1) * LANES] = obuf[i, pl.ds(_unit_row(sq, tc), tc, stride=SUBLANES), :]


def _s5(proj3, state, wts, n_unit, tc):
    nb_all, seqlen, _ = proj3.shape
    nb = SUBLANES * n_unit
    rows = nb * tc
    has_state = state is not None
    if has_state:
        grid = (nb_all // nb,)
        u_map = lambda i: (i, 0, COL_U // B_CH)
        st_map = lambda i: (i, 0)
    else:
        grid = (seqlen // tc,)
        u_map = lambda i: (0, i, COL_U // B_CH)
        st_map = lambda i: (0, 0)
    st_spec = pl.BlockSpec((nb, B_NSTATE), st_map)
    in_specs = [pl.BlockSpec((nb, tc, B_CH), u_map)]
    args = [proj3]
    if has_state:
        in_specs += [st_spec, st_spec]
        args += list(state)
    in_specs += [_const_spec((S5_SLABS, LANES)), _const_spec((S5_SLABS, LANES)),
                 _const_spec((N_SLAB, LANES, 2 * S5_BLK * B_STATE)),
                 _const_spec((N_SLAB, 2 * S5_BLK * B_STATE, LANES)),
                 _const_spec((1, B_CH)), _const_spec((B_CH, B_CH)), _const_spec((1, B_CH))]
    args += list(wts)
    kern = functools.partial(_s5_kernel, n_unit=n_unit, tc=tc, has_state=has_state)
    return pl.pallas_call(
        kern,
        out_shape=[jax.ShapeDtypeStruct((nb_all, seqlen, B_CH), F32),
                   jax.ShapeDtypeStruct((nb_all, B_NSTATE), F32),
                   jax.ShapeDtypeStruct((nb_all, B_NSTATE), F32)],
        grid=grid,
        in_specs=in_specs,
        out_specs=[pl.BlockSpec((nb, tc, B_CH), (lambda i: (i, 0, 0)) if has_state else (lambda i: (0, i, 0))),
                   st_spec, st_spec],
        scratch_shapes=[pltpu.VMEM((N_SLAB, rows, LANES), F32),
                        pltpu.VMEM((2 * S5_SLABS, rows, LANES), F32),
                        pltpu.VMEM((2 * S5_SLABS, nb, LANES), F32),
                        pltpu.VMEM((N_SLAB, rows, LANES), F32)],
        compiler_params=_params(1),
        name="s5_state" if has_state else "s5_prompt",
    )(*args)


def _ff_kernel(*refs, has_mix, has_final):
    refs = list(refs)
    x_ref = refs.pop(0)
    if has_mix:
        ya_ref, yb_ref, wo_ref = refs.pop(0), refs.pop(0), refs.pop(0)
    nw_ref, w1_ref, w2_ref = refs.pop(0), refs.pop(0), refs.pop(0)
    if has_final:
        fw_ref = refs.pop(0)
    (o_ref,) = refs
    x = x_ref[...]
    if has_mix:
        mix = _dot(ya_ref[...].astype(BF16), wo_ref[0:A_D_INNER, :]) \
            + _dot(yb_ref[...].astype(BF16), wo_ref[A_D_INNER:A_D_INNER + B_CH, :])
        x = mix + x
    xn = _rms(x, nw_ref[...]).astype(BF16)
    acc = x
    step = 1024
    for c0 in range(0, D_FF, step):
        h = jnp.maximum(_dot(xn, w1_ref[:, c0:c0 + step]), 0.0)
        acc = acc + _dot((h * h).astype(BF16), w2_ref[c0:c0 + step, :])
    if has_final:
        acc = _rms(acc, fw_ref[...])
    o_ref[...] = acc


def _ff(x2d, mix, nw, w1, w2, final_w):
    m = x2d.shape[0]
    tm = 256
    row_spec = pl.BlockSpec((tm, D_MODEL), lambda i: (i, 0))
    in_specs, args = [row_spec], [x2d]
    if mix is not None:
        ya, yb, wo = mix
        in_specs += [row_spec, row_spec, _const_spec(wo.shape)]
        args += [ya, yb, wo]
    in_specs += [_const_spec((1, D_MODEL)), _const_spec((D_MODEL, D_FF)), _const_spec((D_FF, D_MODEL))]
    args += [nw, w1, w2]
    if final_w is not None:
        in_specs.append(_const_spec((1, D_MODEL)))
        args.append(final_w)
    kern = functools.partial(_ff_kernel, has_mix=mix is not None, has_final=final_w is not None)
    return pl.pallas_call(
        kern,
        out_shape=jax.ShapeDtypeStruct((m, D_MODEL), F32),
        grid=(m // tm,),
        in_specs=in_specs,
        out_specs=row_spec,
        compiler_params=_params(1),
        name="ff_mix" if mix is not None else "ff_final",
    )(*args)


def _conf_kernel(*refs, n_unit, tc, has_state):
    if has_state:
        (x_ref, st_ref, nw_ref, w1_ref, b1_ref, wd_ref, bd_ref, lnw_ref, lnb_ref, w2_ref, b2_ref,
         o_ref, cst_ref, xbuf, wbuf, cvb, obuf) = refs
    else:
        (x_ref, nw_ref, w1_ref, b1_ref, wd_ref, bd_ref, lnw_ref, lnb_ref, w2_ref, b2_ref,
         o_ref, cst_ref, xbuf, wbuf, cvb, obuf) = refs
    nb = SUBLANES * n_unit
    win = (CONV_HIST + tc) * SUBLANES
    hist0 = (CONV_HIST - (C_WIDTH - 1)) * SUBLANES
    n_hist = C_WIDTH - 1
    tile = tc * SUBLANES

    for sq in range(nb):
        for i in range(N_SLAB):
            xbuf[i, pl.ds(_unit_row(sq, tc), tc, stride=SUBLANES), :] = x_ref[sq, :, i * LANES:(i + 1) * LANES]
    x = jnp.concatenate([xbuf[i] for i in range(N_SLAB)], axis=1)
    xn = _rms(x, nw_ref[...]).astype(BF16)
    hcat = _dot(xn, w1_ref[...]) + b1_ref[...]
    h = hcat[:, :C_CH] * jax.nn.sigmoid(hcat[:, C_CH:])

    if has_state:
        for sq in range(nb):
            u, b = divmod(sq, SUBLANES)
            for i in range(N_SLAB):
                wbuf[i, pl.ds(u * win + hist0 + b, n_hist, stride=SUBLANES), :] = st_ref[sq, :, i * LANES:(i + 1) * LANES]
    else:
        @pl.when(pl.program_id(0) == 0)
        def _():
            wbuf[:, 0:CONV_HIST * SUBLANES, :] = jnp.zeros((N_SLAB, CONV_HIST * SUBLANES, LANES), F32)
    for u in range(n_unit):
        for i in range(N_SLAB):
            wbuf[i, u * win + CONV_HIST * SUBLANES:(u + 1) * win, :] = h[u * tile:(u + 1) * tile, i * LANES:(i + 1) * LANES]

    chunk = 8 * SUBLANES
    for i in range(N_SLAB):
        taps = [wd_ref[k:k + 1, i * LANES:(i + 1) * LANES] for k in range(C_WIDTH)]
        bias = bd_ref[:, i * LANES:(i + 1) * LANES]
        for u in range(n_unit):
            def conv_chunk(ch, carry, i=i, u=u, taps=taps, bias=bias):
                base = pl.multiple_of(ch * chunk, chunk)
                acc = jnp.zeros((chunk, LANES), F32)
                for k in range(C_WIDTH):
                    acc = acc + taps[k] * wbuf[i, pl.ds(u * win + hist0 + k * SUBLANES + base, chunk), :]
                cvb[i, pl.ds(u * tile + base, chunk), :] = acc + bias
                return carry
            lax.fori_loop(0, tile // chunk, conv_chunk, 0)

    for sq in range(nb):
        u, b = divmod(sq, SUBLANES)
        for i in range(N_SLAB):
            cst_ref[sq, :, i * LANES:(i + 1) * LANES] = \
                wbuf[i, pl.ds(u * win + (CONV_HIST + tc - n_hist) * SUBLANES + b, n_hist, stride=SUBLANES), :]
    if not has_state:
        wbuf[:, hist0:CONV_HIST * SUBLANES, :] = wbuf[:, hist0 + tile:CONV_HIST * SUBLANES + tile, :]

    c = jnp.concatenate([cvb[i] for i in range(N_SLAB)], axis=1)
    mu = jnp.mean(c, axis=-1, keepdims=True)
    cc = c - mu
    yn = cc * lax.rsqrt(jnp.mean(cc * cc, axis=-1, keepdims=True) + EPS) * lnw_ref[...] + lnb_ref[...]
    act = yn * jax.nn.sigmoid(yn)
    out = (_dot(act.astype(BF16), w2_ref[...]) + b2_ref[...]) + x
    for i in range(N_SLAB):
        obuf[i] = out[:, i * LANES:(i + 1) * LANES]
    for sq in range(nb):
        for i in range(N_SLAB):
            o_ref[sq, :, i * LANES:(i + 1) * LANES] = obuf[i, pl.ds(_unit_row(sq, tc), tc, stride=SUBLANES), :]


def _conformer(x3, state, wts, n_unit, tc):
    nb_all, seqlen, _ = x3.shape
    nb = SUBLANES * n_unit
    rows = nb * tc
    has_state = state is not None
    n_hist = C_WIDTH - 1
    if has_state:
        grid = (nb_all // nb,)
        x_map = lambda i: (i, 0, 0)
        st_map = lambda i: (i, 0, 0)
    else:
        grid = (seqlen // tc,)
        x_map = lambda i: (0, i, 0)
        st_map = lambda i: (0, 0, 0)
    x_spec = pl.BlockSpec((nb, tc, C_CH), x_map)
    st_spec = pl.BlockSpec((nb, n_hist, C_CH), st_map)
    in_specs, args = [x_spec], [x3]
    if has_state:
        in_specs.append(st_spec)
        args.append(state)
    in_specs += [_const_spec((1, D_MODEL)), _const_spec((D_MODEL, 2 * C_CH)), _const_spec((1, 2 * C_CH)),
                 _const_spec((C_WIDTH, C_CH)), _const_spec((1, C_CH)), _const_spec((1, C_CH)), _const_spec((1, C_CH)),
                 _const_spec((C_CH, D_MODEL)), _const_spec((1, D_MODEL))]
    args += list(wts)
    kern = functools.partial(_conf_kernel, n_unit=n_unit, tc=tc, has_state=has_state)
    return pl.pallas_call(
        kern,
        out_shape=[jax.ShapeDtypeStruct((nb_all, seqlen, D_MODEL), F32),
                   jax.ShapeDtypeStruct((nb_all, n_hist, C_CH), F32)],
        grid=grid,
        in_specs=in_specs,
        out_specs=[x_spec, st_spec],
        scratch_shapes=[pltpu.VMEM((N_SLAB, rows, LANES), F32),
                        pltpu.VMEM((N_SLAB, n_unit * (CONV_HIST + tc) * SUBLANES, LANES), F32),
                        pltpu.VMEM((N_SLAB, rows, LANES), F32),
                        pltpu.VMEM((N_SLAB, rows, LANES), F32)],
        compiler_params=_params(1),
        name="conf_state" if has_state else "conf_prompt",
    )(*args)


def _trunk(x, states, w):
    nb, seqlen, _ = x.shape
    has_state = states is not None
    x2 = x.reshape(nb * seqlen, D_MODEL)
    proj = _inproj(x2, w["norm_mix0"], w["w_in"])
    if has_state:
        a_conv, a_ssm, b_re, b_im, c_conv = states
        ya, new_a_conv, new_a_ssm = _mamba_sample(proj, a_conv, a_ssm.reshape(nb, A_D_INNER, A_STATE),
                                                  w["mamba"], nb, seqlen)
        yb, new_b_re, new_b_im = _s5(proj.reshape(nb, seqlen, PROJ_COLS),
                                     (b_re.reshape(nb, B_NSTATE), b_im.reshape(nb, B_NSTATE)), w["s5"],
                                     n_unit=8, tc=seqlen)
    else:
        ya, new_a_conv, new_a_ssm = _mamba_prompt(proj, w["mamba"], nb, seqlen)
        yb, new_b_re, new_b_im = _s5(proj.reshape(nb, seqlen, PROJ_COLS), None, w["s5"], n_unit=1, tc=64)
    x2 = _ff(x2, (ya, yb.reshape(nb * seqlen, B_CH), w["w_out"]), w["norm_ff0"], w["w_ff1_0"], w["w_ff2_0"], None)
    if has_state:
        x3, new_c_conv = _conformer(x2.reshape(nb, seqlen, D_MODEL), c_conv, w["conf"], n_unit=4, tc=seqlen)
    else:
        x3, new_c_conv = _conformer(x2.reshape(nb, seqlen, D_MODEL), None, w["conf"], n_unit=1, tc=64)
    y = _ff(x3.reshape(nb * seqlen, D_MODEL), None, w["norm_ff1"], w["w_ff1_1"], w["w_ff2_1"], w["norm_final"])
    return (y.reshape(nb, seqlen, D_MODEL),
            new_a_conv[None],
            new_a_ssm.reshape(1, nb, A_HEADS, A_HEAD_DIM, A_STATE),
            new_b_re.reshape(1, nb, B_GROUPS, B_STATE),
            new_b_im.reshape(1, nb, B_GROUPS, B_STATE),
            new_c_conv[None])


def _prepare_weights(norm_mix, norm_ff, norm_final, w_in_ab, a_conv_w, a_conv_b, a_dt_bias, a_log, a_d, a_norm,
                     s5_lam_re, s5_lam_im, s5_log_step, s5_b_re, s5_b_im, s5_c_re, s5_c_im, s5_d, s5_w_glu,
                     s5_b_glu, w_out_ab, c_w_pw1, c_b_pw1, c_w_dw, c_b_dw, c_ln_w, c_ln_b, c_w_pw2, c_b_pw2,
                     w_ff1, w_ff2):
    row = lambda v: v.reshape(1, -1)
    a_proj = A_D_INNER + A_CONV_DIM + A_HEADS
    w_in = w_in_ab[0]
    dt_pad = LANES - A_HEADS
    w_in_r = jnp.concatenate([w_in[:, :A_D_INNER + A_CONV_DIM], w_in[:, a_proj:],
                              w_in[:, A_D_INNER + A_CONV_DIM:a_proj],
                              jnp.zeros((D_MODEL, dt_pad), F32)], axis=1).astype(BF16)
    pad_heads = lambda v: jnp.pad(v.reshape(1, A_HEADS), ((0, 0), (0, dt_pad)))
    head_lane = jnp.arange(A_D_INNER) // A_HEAD_DIM
    expand_heads = (jnp.arange(LANES)[:, None] == head_lane[None, :]).astype(BF16)
    mamba = (a_conv_w[0], row(a_conv_b[0]), pad_heads(a_dt_bias[0]), pad_heads(a_log[0]),
             row(jnp.repeat(a_d[0], A_HEAD_DIM)), row(a_norm[0]), expand_heads, expand_heads.T)

    ab_re, ab_im, bb_re, bb_im = _s5_params(s5_lam_re[0], s5_lam_im[0], s5_log_step[0], s5_b_re[0], s5_b_im[0])
    bb = jnp.concatenate([_block_diag(bb_re, S5_BLK), _block_diag(bb_im, S5_BLK)], axis=2).astype(BF16)
    c_re_t = jnp.swapaxes(s5_c_re[0], 1, 2)
    c_im_t = jnp.swapaxes(s5_c_im[0], 1, 2)
    cc = jnp.concatenate([_block_diag(c_re_t, S5_BLK), _block_diag(-c_im_t, S5_BLK)], axis=1).astype(BF16)
    s5 = (ab_re.reshape(S5_SLABS, LANES), ab_im.reshape(S5_SLABS, LANES), bb, cc, row(s5_d[0]),
          s5_w_glu[0].astype(BF16), row(s5_b_glu[0]))

    conf = (row(norm_mix[1]), c_w_pw1[0].astype(BF16), row(c_b_pw1[0]), c_w_dw[0], row(c_b_dw[0]),
            row(c_ln_w[0]), row(c_ln_b[0]), c_w_pw2[0].astype(BF16), row(c_b_pw2[0]))
    return dict(norm_mix0=row(norm_mix[0]), w_in=w_in_r, mamba=mamba, s5=s5, w_out=w_out_ab[0].astype(BF16),
                norm_ff0=row(norm_ff[0]), w_ff1_0=w_ff1[0].astype(BF16), w_ff2_0=w_ff2[0].astype(BF16),
                conf=conf, norm_ff1=row(norm_ff[1]), w_ff1_1=w_ff1[1].astype(BF16), w_ff2_1=w_ff2[1].astype(BF16),
                norm_final=row(norm_final))


def kernel(x_prompt, x_sample, state_a_conv, state_a_ssm, state_b_re, state_b_im, state_c_conv, norm_mix, norm_ff, norm_final, w_in_ab, a_conv_w, a_conv_b, a_dt_bias, a_log, a_d, a_norm, s5_lam_re, s5_lam_im, s5_log_step, s5_b_re, s5_b_im, s5_c_re, s5_c_im, s5_d, s5_w_glu, s5_b_glu, w_out_ab, c_w_pw1, c_b_pw1, c_w_dw, c_b_dw, c_ln_w, c_ln_b, c_w_pw2, c_b_pw2, w_ff1, w_ff2):
    w = _prepare_weights(norm_mix, norm_ff, norm_final, w_in_ab, a_conv_w, a_conv_b, a_dt_bias, a_log, a_d, a_norm,
                         s5_lam_re, s5_lam_im, s5_log_step, s5_b_re, s5_b_im, s5_c_re, s5_c_im, s5_d, s5_w_glu,
                         s5_b_glu, w_out_ab, c_w_pw1, c_b_pw1, c_w_dw, c_b_dw, c_ln_w, c_ln_b, c_w_pw2, c_b_pw2,
                         w_ff1, w_ff2)
    prompt = _trunk(x_prompt, None, w)
    sample = _trunk(x_sample, (state_a_conv[0], state_a_ssm[0], state_b_re[0], state_b_im[0], state_c_conv[0]), w)
    return (prompt[0], sample[0]) + prompt[1:] + sample[1:]
```

```python
import functools
import math

import jax
import jax.numpy as jnp
from jax import lax
from jax.experimental import pallas as pl
from jax.experimental.pallas import tpu as pltpu

F32 = jnp.float32
BF16 = jnp.bfloat16

D_MODEL = 1024
A_HEADS = 16
A_HEAD_DIM = 64
A_D_INNER = A_HEADS * A_HEAD_DIM
A_GROUPS = 4
A_STATE = 128
A_CONV = 4
A_CONV_DIM = A_D_INNER + 2 * A_GROUPS * A_STATE
A_CHUNK = 128
B_CH = 1024
B_GROUP_CH = 16
B_GROUPS = B_CH // B_GROUP_CH
B_STATE = 64
B_NSTATE = B_GROUPS * B_STATE
C_CH = D_MODEL
C_WIDTH = 31
D_FF = 4 * D_MODEL
EPS = 1e-6

SUBLANES = 8
LANES = 128
VMEM_LIMIT_BYTES = 56 * 1024 * 1024

COL_Z = 0
COL_X = A_D_INNER
COL_BC = 2 * A_D_INNER
COL_U = 3 * A_D_INNER
COL_DT = 4 * A_D_INNER
PROJ_COLS = COL_DT + LANES

NEG_BIG = -1e30
N_SLAB = B_CH // LANES
S5_SLABS = B_NSTATE // LANES
S5_BLK = 8
CONV_HIST = 32


def _params(n_grid):
    return pltpu.CompilerParams(dimension_semantics=("arbitrary",) * n_grid,
                                vmem_limit_bytes=VMEM_LIMIT_BYTES)


def _const_spec(shape):
    nd = len(shape)
    return pl.BlockSpec(shape, lambda *_: (0,) * nd, pipeline_mode=pl.Buffered(1))


def _rms(x, w):
    return x * lax.rsqrt(jnp.mean(x * x, axis=-1, keepdims=True) + EPS) * w


def _dot(a, b):
    return jnp.dot(a, b, preferred_element_type=F32)


def _dot_nt(a, b):
    return lax.dot_general(a, b, (((1,), (1,)), ((), ())), preferred_element_type=F32)


def _dot_tn(a, b):
    return lax.dot_general(a, b, (((0,), (0,)), ((), ())), preferred_element_type=F32)


def _split_bf16(v):
    hi = v.astype(BF16)
    lo = (v - hi.astype(F32)).astype(BF16)
    return hi, lo


def _inproj_kernel(x_ref, nw_ref, w_ref, o_ref):
    xn = _rms(x_ref[...], nw_ref[...]).astype(BF16)
    n = o_ref.shape[1]
    step = 512
    for c0 in range(0, n, step):
        c1 = min(c0 + step, n)
        o_ref[:, c0:c1] = _dot(xn, w_ref[:, c0:c1])


def _inproj(x2d, nw, w):
    m = x2d.shape[0]
    tm = 512
    return pl.pallas_call(
        _inproj_kernel,
        out_shape=jax.ShapeDtypeStruct((m, PROJ_COLS), F32),
        grid=(m // tm,),
        in_specs=[pl.BlockSpec((tm, D_MODEL), lambda i: (i, 0)),
                  _const_spec((1, D_MODEL)),
                  _const_spec((D_MODEL, PROJ_COLS))],
        out_specs=pl.BlockSpec((tm, PROJ_COLS), lambda i: (i, 0)),
        compiler_params=_params(1),
        name="inproj",
    )(x2d, nw, w)


def _inproj_conv_kernel(x_ref, nw_ref, w_ref, cw_ref, cb_ref, o_ref, cst_ref, win_ref, *, tiles_per_seq):
    tm = x_ref.shape[0]
    lo = SUBLANES - (A_CONV - 1)

    @pl.when(pl.program_id(0) % tiles_per_seq == 0)
    def _():
        win_ref[...] = jnp.zeros(win_ref.shape, F32)

    xn = _rms(x_ref[...], nw_ref[...]).astype(BF16)
    step = 512
    plain = list(range(COL_Z, COL_X, step)) + list(range(COL_U, PROJ_COLS, step))

    def plain_chunk():
        c0 = plain.pop(0)
        c1 = min(c0 + step, PROJ_COLS)
        o_ref[:, c0:c1] = _dot(xn, w_ref[:, c0:c1])

    for c0 in range(COL_X, COL_U, step):
        w0 = c0 - COL_X
        raw = _dot(xn, w_ref[:, c0:c0 + step])
        plain_chunk()
        full = jnp.concatenate([win_ref[:, w0:w0 + step], raw], axis=0)
        conv = raw * cw_ref[A_CONV - 1:A_CONV, w0:w0 + step]
        for k in range(A_CONV - 1):
            conv = conv + full[lo + k:lo + k + tm] * cw_ref[k:k + 1, w0:w0 + step]
        xc = conv + cb_ref[:, w0:w0 + step]
        o_ref[:, c0:c0 + step] = xc * jax.nn.sigmoid(xc)
        win_ref[:, w0:w0 + step] = raw[tm - SUBLANES:tm]
    cst_ref[0] = win_ref[lo:SUBLANES, :]
    while plain:
        plain_chunk()


def _inproj_conv(x2d, nw, w, cw, cb, nb, seqlen):
    tm = 512
    tiles_per_seq = seqlen // tm
    kern = functools.partial(_inproj_conv_kernel, tiles_per_seq=tiles_per_seq)
    return pl.pallas_call(
        kern,
        out_shape=[jax.ShapeDtypeStruct((nb * seqlen, PROJ_COLS), F32),
                   jax.ShapeDtypeStruct((nb, A_CONV - 1, A_CONV_DIM), F32)],
        grid=(nb * tiles_per_seq,),
        in_specs=[pl.BlockSpec((tm, D_MODEL), lambda i: (i, 0)),
                  _const_spec((1, D_MODEL)),
                  _const_spec((D_MODEL, PROJ_COLS)),
                  _const_spec((A_CONV, A_CONV_DIM)),
                  _const_spec((1, A_CONV_DIM))],
        out_specs=[pl.BlockSpec((tm, PROJ_COLS), lambda i: (i, 0)),
                   pl.BlockSpec((1, A_CONV - 1, A_CONV_DIM), lambda i: (i // tiles_per_seq, 0, 0))],
        scratch_shapes=[pltpu.VMEM((SUBLANES, A_CONV_DIM), F32)],
        compiler_params=_params(1),
        name="inproj_conv",
    )(x2d, nw, w, cw, cb)


def _conv_silu_window(xs_ref, bc_ref, cw_ref, cb_ref, win_ref, *, n_seq, q):
    raw = jnp.concatenate([xs_ref[...], bc_ref[...]], axis=1)
    for s in range(n_seq):
        win_ref[s, SUBLANES:SUBLANES + q, :] = raw[s * q:(s + 1) * q]
    conv = jnp.zeros((n_seq * q, A_CONV_DIM), F32)
    for k in range(A_CONV):
        off = SUBLANES - (A_CONV - 1) + k
        tap = jnp.concatenate([win_ref[s, off:off + q, :] for s in range(n_seq)], axis=0)
        conv = conv + tap * cw_ref[k:k + 1, :]
    xc = conv + cb_ref[...]
    return xc * jax.nn.sigmoid(xc)


def _ssd_tile(xc, z_ref, dt_ref, dtb_ref, alog_ref, dsk_ref, nw_ref, eh_ref, eht_ref,
              h_in_ref, h_out_ref, ya_ref, *, n_seq, q):
    tm = n_seq * q
    x = xc[:, :A_D_INNER]
    bm = xc[:, A_D_INNER:A_D_INNER + A_GROUPS * A_STATE].astype(BF16)
    cm = xc[:, A_D_INNER + A_GROUPS * A_STATE:].astype(BF16)
    xb = x.astype(BF16)

    dt = jax.nn.softplus(dt_ref[...] + dtb_ref[...])
    a = -jnp.exp(alog_ref[...])
    da = dt * a
    ii = lax.broadcasted_iota(jnp.int32, (tm, tm), 0)
    jj = lax.broadcasted_iota(jnp.int32, (tm, tm), 1)
    causal = jj <= ii
    if n_seq > 1:
        causal = causal & ((ii // q) == (jj // q))
    tri = jnp.where(causal, 1.0, 0.0).astype(F32)
    cum = jnp.dot(tri, da, precision=lax.Precision.HIGHEST, preferred_element_type=F32)
    cum_t = cum.T
    dt_t = dt.T
    last = jnp.concatenate(
        [jnp.broadcast_to(cum[s * q + q - 1:s * q + q, :], (q, LANES)) for s in range(n_seq)], axis=0)

    eh = eh_ref[...]

    def expand(v):
        hi, lo = _split_bf16(v)
        return _dot(hi, eh) + _dot(lo, eh)

    exp_cum = expand(jnp.exp(cum))
    w_state = expand(jnp.exp(last - cum) * dt)
    xw = (x * w_state).astype(BF16)
    lane = lax.broadcasted_iota(jnp.int32, (tm, LANES), 1)
    low_half = lane < A_HEAD_DIM

    decs = []
    for s in range(n_seq):
        col = s * q + q - 1
        lb = jnp.broadcast_to(jnp.exp(cum_t[:, col:col + 1]), (LANES, LANES))
        hi, lo = _split_bf16(lb)
        decs.append(_dot(eht_ref[...], hi) + _dot(eht_ref[...], lo))

    heads_per_group = A_HEADS // A_GROUPS
    gw = heads_per_group * A_HEAD_DIM
    y_parts = []
    for g in range(A_GROUPS):
        cg = cm[:, g * A_STATE:(g + 1) * A_STATE]
        bg = bm[:, g * A_STATE:(g + 1) * A_STATE]
        scores = _dot_nt(cg, bg)
        pair_out = []
        for pr in range(heads_per_group // 2):
            lane0 = g * gw + pr * LANES
            xp = xb[:, lane0:lane0 + LANES]
            acc = jnp.zeros((tm, LANES), F32)
            for half in range(2):
                e = g * heads_per_group + pr * 2 + half
                seg = cum[:, e:e + 1] - cum_t[e:e + 1, :]
                dec = jnp.exp(jnp.where(causal, seg, NEG_BIG))
                m = (scores * dec * dt_t[e:e + 1, :]).astype(BF16)
                xh = jnp.where(low_half if half == 0 else jnp.logical_not(low_half), xp, jnp.zeros_like(xp))
                acc = acc + _dot(m, xh)
            pair_out.append(acc)
        y_diag = jnp.concatenate(pair_out, axis=1)
        offs = []
        for s in range(n_seq):
            r0, r1 = s * q, (s + 1) * q
            h_old = h_in_ref[s, g * gw:(g + 1) * gw, :]
            offs.append(_dot_nt(cg[r0:r1], h_old.astype(BF16)))
            st = _dot_tn(xw[r0:r1, g * gw:(g + 1) * gw], bg[r0:r1])
            h_out_ref[s, g * gw:(g + 1) * gw, :] = h_old * decs[s][g * gw:(g + 1) * gw, :] + st
        y_off = jnp.concatenate(offs, axis=0) * exp_cum[:, g * gw:(g + 1) * gw]
        y_parts.append(y_diag + y_off)
    y = jnp.concatenate(y_parts, axis=1) + dsk_ref[...] * x

    z = z_ref[...]
    gt = y * (z * jax.nn.sigmoid(z))
    outs = []
    for g in range(A_GROUPS):
        gg = gt[:, g * gw:(g + 1) * gw]
        outs.append(gg * lax.rsqrt(jnp.mean(gg * gg, axis=-1, keepdims=True) + EPS))
    ya_ref[...] = (jnp.concatenate(outs, axis=1) * nw_ref[...]).astype(ya_ref.dtype)


def _mamba_prompt_kernel(z_ref, xs_ref, bc_ref, dt_ref, dtb_ref, alog_ref, dsk_ref, nw_ref, eh_ref, eht_ref,
                         ya_ref, h_ref):
    @pl.when(pl.program_id(1) == 0)
    def _():
        h_ref[...] = jnp.zeros(h_ref.shape, F32)

    xc = jnp.concatenate([xs_ref[...], bc_ref[...]], axis=1)
    _ssd_tile(xc, z_ref, dt_ref, dtb_ref, alog_ref, dsk_ref, nw_ref, eh_ref, eht_ref,
              h_ref, h_ref, ya_ref, n_seq=1, q=A_CHUNK)


def _mamba_sample_kernel(z_ref, xs_ref, bc_ref, dt_ref, cin_ref, hin_ref, cw_ref, cb_ref, dtb_ref, alog_ref,
                         dsk_ref, nw_ref, eh_ref, eht_ref, ya_ref, cout_ref, h_ref, win_ref, *, n_seq, q):
    lo = SUBLANES - (A_CONV - 1)
    for s in range(n_seq):
        win_ref[s, lo:SUBLANES, :] = cin_ref[s]
    xc = _conv_silu_window(xs_ref, bc_ref, cw_ref, cb_ref, win_ref, n_seq=n_seq, q=q)
    _ssd_tile(xc, z_ref, dt_ref, dtb_ref, alog_ref, dsk_ref, nw_ref, eh_ref, eht_ref,
              hin_ref, h_ref, ya_ref, n_seq=n_seq, q=q)
    for s in range(n_seq):
        cout_ref[s] = win_ref[s, SUBLANES + q - (A_CONV - 1):SUBLANES + q, :]


def _conv_weights_specs():
    return [_const_spec((A_CONV, A_CONV_DIM)), _const_spec((1, A_CONV_DIM))]


def _mamba_weights_specs():
    return [_const_spec((1, LANES)), _const_spec((1, LANES)), _const_spec((1, A_D_INNER)),
            _const_spec((1, A_D_INNER)), _const_spec((LANES, A_D_INNER)), _const_spec((A_D_INNER, LANES))]


def _mamba_prompt(proj, wts, nb, seqlen):
    q = A_CHUNK
    nc = seqlen // q
    wide = A_D_INNER // LANES

    def row(b, c):
        return b * nc + c

    return pl.pallas_call(
        _mamba_prompt_kernel,
        out_shape=[jax.ShapeDtypeStruct((nb * seqlen, A_D_INNER), BF16),
                   jax.ShapeDtypeStruct((nb, A_D_INNER, A_STATE), F32)],
        grid=(nb, nc),
        in_specs=[pl.BlockSpec((q, A_D_INNER), lambda b, c: (row(b, c), 0)),
                  pl.BlockSpec((q, A_D_INNER), lambda b, c: (row(b, c), 1)),
                  pl.BlockSpec((q, A_D_INNER), lambda b, c: (row(b, c), 2)),
                  pl.BlockSpec((q, LANES), lambda b, c: (row(b, c), 4 * wide))] + _mamba_weights_specs(),
        out_specs=[pl.BlockSpec((q, A_D_INNER), lambda b, c: (row(b, c), 0)),
                   pl.BlockSpec((1, A_D_INNER, A_STATE), lambda b, c: (b, 0, 0))],
        compiler_params=_params(2),
        name="mamba_prompt",
    )(proj, proj, proj, proj, *wts)


def _mamba_sample(proj, conv_state, ssm_state, wts, nb, seqlen):
    n_seq = 8
    q = seqlen
    tm = n_seq * q
    wide = A_D_INNER // LANES
    kern = functools.partial(_mamba_sample_kernel, n_seq=n_seq, q=q)
    return pl.pallas_call(
        kern,
        out_shape=[jax.ShapeDtypeStruct((nb * seqlen, A_D_INNER), BF16),
                   jax.ShapeDtypeStruct((nb, A_CONV - 1, A_CONV_DIM), F32),
                   jax.ShapeDtypeStruct((nb, A_D_INNER, A_STATE), F32)],
        grid=(nb // n_seq,),
        in_specs=[pl.BlockSpec((tm, A_D_INNER), lambda i: (i, 0)),
                  pl.BlockSpec((tm, A_D_INNER), lambda i: (i, 1)),
                  pl.BlockSpec((tm, A_D_INNER), lambda i: (i, 2)),
                  pl.BlockSpec((tm, LANES), lambda i: (i, 4 * wide)),
                  pl.BlockSpec((n_seq, A_CONV - 1, A_CONV_DIM), lambda i: (i, 0, 0)),
                  pl.BlockSpec((n_seq, A_D_INNER, A_STATE), lambda i: (i, 0, 0))]
                 + _conv_weights_specs() + _mamba_weights_specs(),
        out_specs=[pl.BlockSpec((tm, A_D_INNER), lambda i: (i, 0)),
                   pl.BlockSpec((n_seq, A_CONV - 1, A_CONV_DIM), lambda i: (i, 0, 0)),
                   pl.BlockSpec((n_seq, A_D_INNER, A_STATE), lambda i: (i, 0, 0))],
        scratch_shapes=[pltpu.VMEM((n_seq, SUBLANES + q, A_CONV_DIM), F32)],
        compiler_params=_params(1),
        name="mamba_sample",
    )(proj, proj, proj, proj, conv_state, ssm_state, *wts)


def _s5_param_kernel(lr_ref, li_ref, ls_ref, br_ref, bi_ref, are_ref, aim_ref, bbr_ref, bbi_ref):
    lr = lr_ref[...]
    li = li_ref[...]
    step = jnp.exp(ls_ref[...])
    mag = jnp.exp(lr * step)
    ab_re = mag * jnp.cos(li * step)
    ab_im = mag * jnp.sin(li * step)
    den = lr * lr + li * li
    k_re = ((ab_re - 1.0) * lr + ab_im * li) / den
    k_im = (ab_im * lr - (ab_re - 1.0) * li) / den
    br = br_ref[...]
    bi = bi_ref[...]
    are_ref[...] = ab_re
    aim_ref[...] = ab_im
    bbr_ref[...] = k_re * br - k_im * bi
    bbi_ref[...] = k_re * bi + k_im * br


def _s5_params(lam_re, lam_im, log_step, b_re, b_im):
    g, p, c = B_GROUPS, B_STATE, B_GROUP_CH
    return pl.pallas_call(
        _s5_param_kernel,
        out_shape=[jax.ShapeDtypeStruct((g, 1, p), F32), jax.ShapeDtypeStruct((g, 1, p), F32),
                   jax.ShapeDtypeStruct((g, c, p), F32), jax.ShapeDtypeStruct((g, c, p), F32)],
        name="s5_params",
    )(lam_re.reshape(g, 1, p), lam_im.reshape(g, 1, p), log_step.reshape(g, 1, 1),
      jnp.swapaxes(b_re, 1, 2), jnp.swapaxes(b_im, 1, 2))


def _block_diag(m, blk):
    g, r, c = m.shape
    eye = jnp.eye(blk, dtype=m.dtype)
    t = m.reshape(g // blk, blk, r, c)
    return (t[:, :, :, None, :] * eye[None, :, None, :, None]).reshape(g // blk, blk * r, blk * c)


def _gelu_tanh(x):
    return x * (0.5 * (1.0 + jnp.tanh(math.sqrt(2.0 / math.pi) * (x + 0.044715 * (x * x * x)))))


def _unit_row(sq, tc):
    u, b = divmod(sq, SUBLANES)
    return u * tc * SUBLANES + b


def _s5_kernel(*refs, n_unit, tc, has_state):
    if has_state:
        (u_ref, sre_ref, sim_ref, are_ref, aim_ref, bb_ref, cc_ref, dsk_ref, wg_ref, bg_ref,
         o_ref, hre_ref, him_ref, ubuf, buf, hst, obuf) = refs
    else:
        (u_ref, are_ref, aim_ref, bb_ref, cc_ref, dsk_ref, wg_ref, bg_ref,
         o_ref, hre_ref, him_ref, ubuf, buf, hst, obuf) = refs
    nb = SUBLANES * n_unit
    half = S5_SLABS

    for sq in range(nb):
        for i in range(N_SLAB):
            ubuf[i, pl.ds(_unit_row(sq, tc), tc, stride=SUBLANES), :] = u_ref[sq, :, i * LANES:(i + 1) * LANES]

    if has_state:
        for j in range(S5_SLABS):
            hst[j] = sre_ref[:, j * LANES:(j + 1) * LANES]
            hst[half + j] = sim_ref[:, j * LANES:(j + 1) * LANES]
    else:
        @pl.when(pl.program_id(0) == 0)
        def _():
            hst[...] = jnp.zeros(hst.shape, F32)

    per = S5_SLABS // N_SLAB

    def b_proj(i):
        r = _dot(ubuf[i].astype(BF16), bb_ref[i])
        for k in range(per):
            buf[per * i + k] = r[:, k * LANES:(k + 1) * LANES]
            buf[half + per * i + k] = r[:, (per + k) * LANES:(per + k + 1) * LANES]

    def scan(i):
        a_re = [jnp.broadcast_to(are_ref[per * i + k:per * i + k + 1, :], (SUBLANES, LANES)) for k in range(per)]
        a_im = [jnp.broadcast_to(aim_ref[per * i + k:per * i + k + 1, :], (SUBLANES, LANES)) for k in range(per)]
        for u in range(n_unit):
            rows = slice(u * SUBLANES, (u + 1) * SUBLANES)
            h_re = [hst[per * i + k, rows, :] for k in range(per)]
            h_im = [hst[half + per * i + k, rows, :] for k in range(per)]
            for t in range(tc):
                r0 = (u * tc + t) * SUBLANES
                for k in range(per):
                    j = per * i + k
                    n_re = a_re[k] * h_re[k] - a_im[k] * h_im[k] + buf[j, r0:r0 + SUBLANES, :]
                    n_im = a_re[k] * h_im[k] + a_im[k] * h_re[k] + buf[half + j, r0:r0 + SUBLANES, :]
                    buf[j, r0:r0 + SUBLANES, :] = n_re
                    buf[half + j, r0:r0 + SUBLANES, :] = n_im
                    h_re[k], h_im[k] = n_re, n_im
            for k in range(per):
                hst[per * i + k, rows, :] = h_re[k]
                hst[half + per * i + k, rows, :] = h_im[k]

    def c_proj(i):
        lhs = jnp.concatenate([buf[per * i + k] for k in range(per)]
                              + [buf[half + per * i + k] for k in range(per)], axis=1).astype(BF16)
        yi = _dot(lhs, cc_ref[i]) + dsk_ref[:, i * LANES:(i + 1) * LANES] * ubuf[i]
        return _gelu_tanh(yi)

    ys = []
    b_proj(0)
    for i in range(N_SLAB):
        if i + 1 < N_SLAB:
            b_proj(i + 1)
        scan(i)
        ys.append(c_proj(i))

    for j in range(S5_SLABS):
        hre_ref[:, j * LANES:(j + 1) * LANES] = hst[j]
        him_ref[:, j * LANES:(j + 1) * LANES] = hst[half + j]

    y = jnp.concatenate(ys, axis=1)
    out = y * jax.nn.sigmoid(_dot(y.astype(BF16), wg_ref[...]) + bg_ref[...])
    for i in range(N_SLAB):
        obuf[i] = out[:, i * LANES:(i + 1) * LANES]
    for sq in range(nb):
        for i in range(N_SLAB):
            o_ref[sq, :, i * LANES:(i + 1) * LANES] = \
                obuf[i, pl.ds(_unit_row(sq, tc), tc, stride=SUBLANES), :].astype(o_ref.dtype)


def _s5(proj3, state, wts, n_unit, tc):
    nb_all, seqlen, _ = proj3.shape
    nb = SUBLANES * n_unit
    rows = nb * tc
    has_state = state is not None
    if has_state:
        grid = (nb_all // nb,)
        u_map = lambda i: (i, 0, COL_U // B_CH)
        st_map = lambda i: (i, 0)
    else:
        grid = (seqlen // tc,)
        u_map = lambda i: (0, i, COL_U // B_CH)
        st_map = lambda i: (0, 0)
    st_spec = pl.BlockSpec((nb, B_NSTATE), st_map)
    in_specs = [pl.BlockSpec((nb, tc, B_CH), u_map)]
    args = [proj3]
    if has_state:
        in_specs += [st_spec, st_spec]
        args += list(state)
    in_specs += [_const_spec((S5_SLABS, LANES)), _const_spec((S5_SLABS, LANES)),
                 _const_spec((N_SLAB, LANES, 2 * S5_BLK * B_STATE)),
                 _const_spec((N_SLAB, 2 * S5_BLK * B_STATE, LANES)),
                 _const_spec((1, B_CH)), _const_spec((B_CH, B_CH)), _const_spec((1, B_CH))]
    args += list(wts)
    kern = functools.partial(_s5_kernel, n_unit=n_unit, tc=tc, has_state=has_state)
    return pl.pallas_call(
        kern,
        out_shape=[jax.ShapeDtypeStruct((nb_all, seqlen, B_CH), BF16 if tc % (2 * SUBLANES) == 0 else F32),
                   jax.ShapeDtypeStruct((nb_all, B_NSTATE), F32),
                   jax.ShapeDtypeStruct((nb_all, B_NSTATE), F32)],
        grid=grid,
        in_specs=in_specs,
        out_specs=[pl.BlockSpec((nb, tc, B_CH), (lambda i: (i, 0, 0)) if has_state else (lambda i: (0, i, 0))),
                   st_spec, st_spec],
        scratch_shapes=[pltpu.VMEM((N_SLAB, rows, LANES), F32),
                        pltpu.VMEM((2 * S5_SLABS, rows, LANES), F32),
                        pltpu.VMEM((2 * S5_SLABS, nb, LANES), F32),
                        pltpu.VMEM((N_SLAB, rows, LANES), F32)],
        compiler_params=_params(1),
        name="s5_state" if has_state else "s5_prompt",
    )(*args)


def _ff_kernel(*refs, has_mix, has_final):
    refs = list(refs)
    x_ref = refs.pop(0)
    if has_mix:
        ya_ref, yb_ref, wo_ref = refs.pop(0), refs.pop(0), refs.pop(0)
    nw_ref, w1_ref, w2_ref = refs.pop(0), refs.pop(0), refs.pop(0)
    if has_final:
        fw_ref = refs.pop(0)
    (o_ref,) = refs
    x = x_ref[...]
    if has_mix:
        mix = _dot(ya_ref[...].astype(BF16), wo_ref[0:A_D_INNER, :]) \
            + _dot(yb_ref[...].astype(BF16), wo_ref[A_D_INNER:A_D_INNER + B_CH, :])
        x = mix + x
    xn = _rms(x, nw_ref[...]).astype(BF16)
    acc = x
    step = 1024
    for c0 in range(0, D_FF, step):
        h = jnp.maximum(_dot(xn, w1_ref[:, c0:c0 + step]), 0.0)
        acc = acc + _dot((h * h).astype(BF16), w2_ref[c0:c0 + step, :])
    if has_final:
        acc = _rms(acc, fw_ref[...])
    o_ref[...] = acc


def _layer_spec(shape, layer):
    return pl.BlockSpec((None,) + tuple(shape), lambda *_: (layer, 0, 0), pipeline_mode=pl.Buffered(1))


def _ff(x2d, mix, nw, w1, w2, layer, final_w):
    m = x2d.shape[0]
    tm = 512
    row_spec = pl.BlockSpec((tm, D_MODEL), lambda i: (i, 0))
    in_specs, args = [row_spec], [x2d]
    if mix is not None:
        ya, yb, wo = mix
        in_specs += [row_spec, row_spec, _const_spec(wo.shape)]
        args += [ya, yb, wo]
    in_specs += [_const_spec((1, D_MODEL)), _layer_spec((D_MODEL, D_FF), layer), _layer_spec((D_FF, D_MODEL), layer)]
    args += [nw, w1, w2]
    if final_w is not None:
        in_specs.append(_const_spec((1, D_MODEL)))
        args.append(final_w)
    kern = functools.partial(_ff_kernel, has_mix=mix is not None, has_final=final_w is not None)
    return pl.pallas_call(
        kern,
        out_shape=jax.ShapeDtypeStruct((m, D_MODEL), F32),
        grid=(m // tm,),
        in_specs=in_specs,
        out_specs=row_spec,
        compiler_params=_params(1),
        name="ff_mix" if mix is not None else "ff_final",
    )(*args)


def _conf_kernel(*refs, n_unit, tc, has_state):
    if has_state:
        (x_ref, st_ref, nw_ref, w1_ref, b1_ref, wd_ref, bd_ref, lnw_ref, lnb_ref, w2_ref, b2_ref,
         o_ref, cst_ref, xbuf, wbuf, cvb, obuf) = refs
    else:
        (x_ref, nw_ref, w1_ref, b1_ref, wd_ref, bd_ref, lnw_ref, lnb_ref, w2_ref, b2_ref,
         o_ref, cst_ref, xbuf, wbuf, cvb, obuf) = refs
    nb = SUBLANES * n_unit
    win = (CONV_HIST + tc) * SUBLANES
    hist0 = (CONV_HIST - (C_WIDTH - 1)) * SUBLANES
    n_hist = C_WIDTH - 1
    tile = tc * SUBLANES

    for sq in range(nb):
        for i in range(N_SLAB):
            xbuf[i, pl.ds(_unit_row(sq, tc), tc, stride=SUBLANES), :] = x_ref[sq, :, i * LANES:(i + 1) * LANES]
    x = jnp.concatenate([xbuf[i] for i in range(N_SLAB)], axis=1)
    xn = _rms(x, nw_ref[...]).astype(BF16)

    if has_state:
        for u in range(n_unit):
            for i in range(N_SLAB):
                wbuf[i, u * win + hist0:u * win + hist0 + n_hist * SUBLANES, :] = \
                    st_ref[:, u * SUBLANES:(u + 1) * SUBLANES, i * LANES:(i + 1) * LANES].reshape(n_hist * SUBLANES, LANES)
    else:
        @pl.when(pl.program_id(0) == 0)
        def _():
            wbuf[:, 0:CONV_HIST * SUBLANES, :] = jnp.zeros((N_SLAB, CONV_HIST * SUBLANES, LANES), F32)

    chunk = 8 * SUBLANES
    pair = 2 * LANES
    for p in range(N_SLAB // 2):
        val = _dot(xn, w1_ref[:, p * pair:(p + 1) * pair]) + b1_ref[:, p * pair:(p + 1) * pair]
        gate = _dot(xn, w1_ref[:, C_CH + p * pair:C_CH + (p + 1) * pair]) + b1_ref[:, C_CH + p * pair:C_CH + (p + 1) * pair]
        h = val * jax.nn.sigmoid(gate)
        for i in (2 * p, 2 * p + 1):
            for u in range(n_unit):
                wbuf[i, u * win + CONV_HIST * SUBLANES:(u + 1) * win, :] = \
                    h[u * tile:(u + 1) * tile, (i - 2 * p) * LANES:(i - 2 * p + 1) * LANES]
        for i in (2 * p, 2 * p + 1):
            taps = [wd_ref[k:k + 1, i * LANES:(i + 1) * LANES] for k in range(C_WIDTH)]
            bias = bd_ref[:, i * LANES:(i + 1) * LANES]
            for u in range(n_unit):
                for base in range(0, tile, chunk):
                    r0 = u * win + hist0 + base
                    acc = taps[0] * wbuf[i, r0:r0 + chunk, :]
                    for k in range(1, C_WIDTH):
                        acc = acc + taps[k] * wbuf[i, r0 + k * SUBLANES:r0 + k * SUBLANES + chunk, :]
                    cvb[i, u * tile + base:u * tile + base + chunk, :] = acc + bias

    for u in range(n_unit):
        tail = u * win + (CONV_HIST + tc - n_hist) * SUBLANES
        for i in range(N_SLAB):
            cst_ref[:, u * SUBLANES:(u + 1) * SUBLANES, i * LANES:(i + 1) * LANES] = \
                wbuf[i, tail:tail + n_hist * SUBLANES, :].reshape(n_hist, SUBLANES, LANES)
    if not has_state:
        wbuf[:, hist0:CONV_HIST * SUBLANES, :] = wbuf[:, hist0 + tile:CONV_HIST * SUBLANES + tile, :]

    c = jnp.concatenate([cvb[i] for i in range(N_SLAB)], axis=1)
    mu = jnp.mean(c, axis=-1, keepdims=True)
    cc = c - mu
    yn = cc * lax.rsqrt(jnp.mean(cc * cc, axis=-1, keepdims=True) + EPS) * lnw_ref[...] + lnb_ref[...]
    act = yn * jax.nn.sigmoid(yn)
    out = (_dot(act.astype(BF16), w2_ref[...]) + b2_ref[...]) + x
    for i in range(N_SLAB):
        obuf[i] = out[:, i * LANES:(i + 1) * LANES]
    for sq in range(nb):
        for i in range(N_SLAB):
            o_ref[sq, :, i * LANES:(i + 1) * LANES] = obuf[i, pl.ds(_unit_row(sq, tc), tc, stride=SUBLANES), :]


def _conformer(x3, state, wts, n_unit, tc):
    nb_all, seqlen, _ = x3.shape
    nb = SUBLANES * n_unit
    rows = nb * tc
    has_state = state is not None
    n_hist = C_WIDTH - 1
    if has_state:
        grid = (nb_all // nb,)
        x_map = lambda i: (i, 0, 0)
        st_map = lambda i: (0, i, 0)
    else:
        grid = (seqlen // tc,)
        x_map = lambda i: (0, i, 0)
        st_map = lambda i: (0, 0, 0)
    x_spec = pl.BlockSpec((nb, tc, C_CH), x_map)
    st_spec = pl.BlockSpec((n_hist, nb, C_CH), st_map)
    in_specs, args = [x_spec], [x3]
    if has_state:
        in_specs.append(st_spec)
        args.append(state)
    in_specs += [_const_spec((1, D_MODEL)), _const_spec((D_MODEL, 2 * C_CH)), _const_spec((1, 2 * C_CH)),
                 _const_spec((C_WIDTH, C_CH)), _const_spec((1, C_CH)), _const_spec((1, C_CH)), _const_spec((1, C_CH)),
                 _const_spec((C_CH, D_MODEL)), _const_spec((1, D_MODEL))]
    args += list(wts)
    kern = functools.partial(_conf_kernel, n_unit=n_unit, tc=tc, has_state=has_state)
    return pl.pallas_call(
        kern,
        out_shape=[jax.ShapeDtypeStruct((nb_all, seqlen, D_MODEL), F32),
                   jax.ShapeDtypeStruct((n_hist, nb_all, C_CH), F32)],
        grid=grid,
        in_specs=in_specs,
        out_specs=[x_spec, st_spec],
        scratch_shapes=[pltpu.VMEM((N_SLAB, rows, LANES), F32),
                        pltpu.VMEM((N_SLAB, n_unit * (CONV_HIST + tc) * SUBLANES, LANES), F32),
                        pltpu.VMEM((N_SLAB, rows, LANES), F32),
                        pltpu.VMEM((N_SLAB, rows, LANES), F32)],
        compiler_params=_params(1),
        name="conf_state" if has_state else "conf_prompt",
    )(*args)


def _trunk(x, states, w):
    nb, seqlen, _ = x.shape
    has_state = states is not None
    x2 = x.reshape(nb * seqlen, D_MODEL)
    if has_state:
        a_conv, a_ssm, b_re, b_im, c_conv = states
        proj = _inproj(x2, w["norm_mix0"], w["w_in"])
        ya, new_a_conv, new_a_ssm = _mamba_sample(proj, a_conv, a_ssm.reshape(nb, A_D_INNER, A_STATE),
                                                  w["a_conv"] + w["mamba"], nb, seqlen)
        yb, new_b_re, new_b_im = _s5(proj.reshape(nb, seqlen, PROJ_COLS),
                                     (b_re.reshape(nb, B_NSTATE), b_im.reshape(nb, B_NSTATE)), w["s5"],
                                     n_unit=8, tc=seqlen)
    else:
        proj, new_a_conv = _inproj_conv(x2, w["norm_mix0"], w["w_in"], *w["a_conv"], nb, seqlen)
        ya, new_a_ssm = _mamba_prompt(proj, w["mamba"], nb, seqlen)
        yb, new_b_re, new_b_im = _s5(proj.reshape(nb, seqlen, PROJ_COLS), None, w["s5"], n_unit=1, tc=64)
    x2 = _ff(x2, (ya, yb.reshape(nb * seqlen, B_CH), w["w_out"]), w["norm_ff0"], w["w_ff1"], w["w_ff2"], 0, None)
    if has_state:
        x3, new_c_conv = _conformer(x2.reshape(nb, seqlen, D_MODEL), c_conv, w["conf"], n_unit=4, tc=seqlen)
    else:
        x3, new_c_conv = _conformer(x2.reshape(nb, seqlen, D_MODEL), None, w["conf"], n_unit=1, tc=64)
    y = _ff(x3.reshape(nb * seqlen, D_MODEL), None, w["norm_ff1"], w["w_ff1"], w["w_ff2"], 1, w["norm_final"])
    return (y.reshape(nb, seqlen, D_MODEL),
            new_a_conv[None],
            new_a_ssm.reshape(1, nb, A_HEADS, A_HEAD_DIM, A_STATE),
            new_b_re.reshape(1, nb, B_GROUPS, B_STATE),
            new_b_im.reshape(1, nb, B_GROUPS, B_STATE),
            jnp.swapaxes(new_c_conv, 0, 1)[None])


def _prepare_weights(norm_mix, norm_ff, norm_final, w_in_ab, a_conv_w, a_conv_b, a_dt_bias, a_log, a_d, a_norm,
                     s5_lam_re, s5_lam_im, s5_log_step, s5_b_re, s5_b_im, s5_c_re, s5_c_im, s5_d, s5_w_glu,
                     s5_b_glu, w_out_ab, c_w_pw1, c_b_pw1, c_w_dw, c_b_dw, c_ln_w, c_ln_b, c_w_pw2, c_b_pw2,
                     w_ff1, w_ff2):
    row = lambda v: v.reshape(1, -1)
    a_proj = A_D_INNER + A_CONV_DIM + A_HEADS
    w_in = w_in_ab[0]
    dt_pad = LANES - A_HEADS
    w_in_r = jnp.concatenate([w_in[:, :A_D_INNER + A_CONV_DIM], w_in[:, a_proj:],
                              w_in[:, A_D_INNER + A_CONV_DIM:a_proj],
                              jnp.zeros((D_MODEL, dt_pad), F32)], axis=1).astype(BF16)
    pad_heads = lambda v: jnp.pad(v.reshape(1, A_HEADS), ((0, 0), (0, dt_pad)))
    head_lane = jnp.arange(A_D_INNER) // A_HEAD_DIM
    expand_heads = (jnp.arange(LANES)[:, None] == head_lane[None, :]).astype(BF16)
    a_conv = (a_conv_w[0], row(a_conv_b[0]))
    mamba = (pad_heads(a_dt_bias[0]), pad_heads(a_log[0]),
             row(jnp.repeat(a_d[0], A_HEAD_DIM)), row(a_norm[0]), expand_heads, expand_heads.T)

    ab_re, ab_im, bb_re, bb_im = _s5_params(s5_lam_re[0], s5_lam_im[0], s5_log_step[0], s5_b_re[0], s5_b_im[0])
    bb = jnp.concatenate([_block_diag(bb_re, S5_BLK), _block_diag(bb_im, S5_BLK)], axis=2).astype(BF16)
    c_re_t = jnp.swapaxes(s5_c_re[0], 1, 2)
    c_im_t = jnp.swapaxes(s5_c_im[0], 1, 2)
    cc = jnp.concatenate([_block_diag(c_re_t, S5_BLK), _block_diag(-c_im_t, S5_BLK)], axis=1).astype(BF16)
    s5 = (ab_re.reshape(S5_SLABS, LANES), ab_im.reshape(S5_SLABS, LANES), bb, cc, row(s5_d[0]),
          s5_w_glu[0].astype(BF16), row(s5_b_glu[0]))

    conf = (row(norm_mix[1]), c_w_pw1[0].astype(BF16), row(c_b_pw1[0]), c_w_dw[0], row(c_b_dw[0]),
            row(c_ln_w[0]), row(c_ln_b[0]), c_w_pw2[0].astype(BF16), row(c_b_pw2[0]))
    return dict(norm_mix0=row(norm_mix[0]), w_in=w_in_r, a_conv=a_conv, mamba=mamba, s5=s5, w_out=w_out_ab[0].astype(BF16),
                norm_ff0=row(norm_ff[0]), norm_ff1=row(norm_ff[1]), w_ff1=w_ff1.astype(BF16), w_ff2=w_ff2.astype(BF16),
                conf=conf, norm_final=row(norm_final))


def kernel(x_prompt, x_sample, state_a_conv, state_a_ssm, state_b_re, state_b_im, state_c_conv, norm_mix, norm_ff, norm_final, w_in_ab, a_conv_w, a_conv_b, a_dt_bias, a_log, a_d, a_norm, s5_lam_re, s5_lam_im, s5_log_step, s5_b_re, s5_b_im, s5_c_re, s5_c_im, s5_d, s5_w_glu, s5_b_glu, w_out_ab, c_w_pw1, c_b_pw1, c_w_dw, c_b_dw, c_ln_w, c_ln_b, c_w_pw2, c_b_pw2, w_ff1, w_ff2):
    w = _prepare_weights(norm_mix, norm_ff, norm_final, w_in_ab, a_conv_w, a_conv_b, a_dt_bias, a_log, a_d, a_norm,
                         s5_lam_re, s5_lam_im, s5_log_step, s5_b_re, s5_b_im, s5_c_re, s5_c_im, s5_d, s5_w_glu,
                         s5_b_glu, w_out_ab, c_w_pw1, c_b_pw1, c_w_dw, c_b_dw, c_ln_w, c_ln_b, c_w_pw2, c_b_pw2,
                         w_ff1, w_ff2)
    prompt = _trunk(x_prompt, None, w)
    sample = _trunk(x_sample, (state_a_conv[0], state_a_ssm[0], state_b_re[0], state_b_im[0],
                               jnp.swapaxes(state_c_conv[0], 0, 1)), w)
    return (prompt[0], sample[0]) + prompt[1:] + sample[1:]
```

```python
import functools
import math

import jax
import jax.numpy as jnp
from jax import lax
from jax.experimental import pallas as pl
from jax.experimental.pallas import tpu as pltpu

F32 = jnp.float32
BF16 = jnp.bfloat16

D_MODEL = 1024
A_HEADS = 16
A_HEAD_DIM = 64
A_D_INNER = A_HEADS * A_HEAD_DIM
A_GROUPS = 4
A_STATE = 128
A_CONV = 4
A_CONV_DIM = A_D_INNER + 2 * A_GROUPS * A_STATE
A_CHUNK = 128
B_CH = 1024
B_GROUP_CH = 16
B_GROUPS = B_CH // B_GROUP_CH
B_STATE = 64
B_NSTATE = B_GROUPS * B_STATE
C_CH = D_MODEL
C_WIDTH = 31
D_FF = 4 * D_MODEL
EPS = 1e-6

SUBLANES = 8
LANES = 128
VMEM_LIMIT_BYTES = 56 * 1024 * 1024

COL_Z = 0
COL_X = A_D_INNER
COL_BC = 2 * A_D_INNER
COL_U = 3 * A_D_INNER
COL_DT = 4 * A_D_INNER
PROJ_COLS = COL_DT + LANES

NEG_BIG = -1e30
N_SLAB = B_CH // LANES
S5_SLABS = B_NSTATE // LANES
S5_BLK = 8
CONV_HIST = 32


def _params(n_grid):
    return pltpu.CompilerParams(dimension_semantics=("arbitrary",) * n_grid,
                                vmem_limit_bytes=VMEM_LIMIT_BYTES)


def _const_spec(shape):
    nd = len(shape)
    return pl.BlockSpec(shape, lambda *_: (0,) * nd, pipeline_mode=pl.Buffered(1))


def _rms(x, w):
    return x * lax.rsqrt(jnp.mean(x * x, axis=-1, keepdims=True) + EPS) * w


def _dot(a, b):
    return jnp.dot(a, b, preferred_element_type=F32)


def _dot_nt(a, b):
    return lax.dot_general(a, b, (((1,), (1,)), ((), ())), preferred_element_type=F32)


def _dot_tn(a, b):
    return lax.dot_general(a, b, (((0,), (0,)), ((), ())), preferred_element_type=F32)


def _split_bf16(v):
    hi = v.astype(BF16)
    lo = (v - hi.astype(F32)).astype(BF16)
    return hi, lo


def _inproj_kernel(x_ref, nw_ref, w_ref, o_ref):
    xn = _rms(x_ref[...], nw_ref[...]).astype(BF16)
    n = o_ref.shape[1]
    step = 512
    for c0 in range(0, n, step):
        c1 = min(c0 + step, n)
        o_ref[:, c0:c1] = _dot(xn, w_ref[:, c0:c1])


def _inproj(x2d, nw, w):
    m = x2d.shape[0]
    tm = 512
    return pl.pallas_call(
        _inproj_kernel,
        out_shape=jax.ShapeDtypeStruct((m, PROJ_COLS), F32),
        grid=(m // tm,),
        in_specs=[pl.BlockSpec((tm, D_MODEL), lambda i: (i, 0)),
                  _const_spec((1, D_MODEL)),
                  _const_spec((D_MODEL, PROJ_COLS))],
        out_specs=pl.BlockSpec((tm, PROJ_COLS), lambda i: (i, 0)),
        compiler_params=_params(1),
        name="inproj",
    )(x2d, nw, w)


def _inproj_conv_kernel(x_ref, nw_ref, w_ref, cw_ref, cb_ref, o_ref, cst_ref, win_ref, *, tiles_per_seq):
    tm = x_ref.shape[0]
    lo = SUBLANES - (A_CONV - 1)

    @pl.when(pl.program_id(0) % tiles_per_seq == 0)
    def _():
        win_ref[...] = jnp.zeros(win_ref.shape, F32)

    xn = _rms(x_ref[...], nw_ref[...]).astype(BF16)
    step = 512
    plain = list(range(COL_Z, COL_X, step)) + list(range(COL_U, PROJ_COLS, step))

    def plain_chunk():
        c0 = plain.pop(0)
        c1 = min(c0 + step, PROJ_COLS)
        o_ref[:, c0:c1] = _dot(xn, w_ref[:, c0:c1])

    for c0 in range(COL_X, COL_U, step):
        w0 = c0 - COL_X
        raw = _dot(xn, w_ref[:, c0:c0 + step])
        plain_chunk()
        full = jnp.concatenate([win_ref[:, w0:w0 + step], raw], axis=0)
        conv = raw * cw_ref[A_CONV - 1:A_CONV, w0:w0 + step]
        for k in range(A_CONV - 1):
            conv = conv + full[lo + k:lo + k + tm] * cw_ref[k:k + 1, w0:w0 + step]
        xc = conv + cb_ref[:, w0:w0 + step]
        o_ref[:, c0:c0 + step] = xc * jax.nn.sigmoid(xc)
        win_ref[:, w0:w0 + step] = raw[tm - SUBLANES:tm]
    cst_ref[0] = win_ref[lo:SUBLANES, :]
    while plain:
        plain_chunk()


def _inproj_conv(x2d, nw, w, cw, cb, nb, seqlen):
    tm = 512
    tiles_per_seq = seqlen // tm
    kern = functools.partial(_inproj_conv_kernel, tiles_per_seq=tiles_per_seq)
    return pl.pallas_call(
        kern,
        out_shape=[jax.ShapeDtypeStruct((nb * seqlen, PROJ_COLS), F32),
                   jax.ShapeDtypeStruct((nb, A_CONV - 1, A_CONV_DIM), F32)],
        grid=(nb * tiles_per_seq,),
        in_specs=[pl.BlockSpec((tm, D_MODEL), lambda i: (i, 0)),
                  _const_spec((1, D_MODEL)),
                  _const_spec((D_MODEL, PROJ_COLS)),
                  _const_spec((A_CONV, A_CONV_DIM)),
                  _const_spec((1, A_CONV_DIM))],
        out_specs=[pl.BlockSpec((tm, PROJ_COLS), lambda i: (i, 0)),
                   pl.BlockSpec((1, A_CONV - 1, A_CONV_DIM), lambda i: (i // tiles_per_seq, 0, 0))],
        scratch_shapes=[pltpu.VMEM((SUBLANES, A_CONV_DIM), F32)],
        compiler_params=_params(1),
        name="inproj_conv",
    )(x2d, nw, w, cw, cb)


def _conv_silu_window(xs_ref, bc_ref, cw_ref, cb_ref, win_ref, *, n_seq, q):
    raw = jnp.concatenate([xs_ref[...], bc_ref[...]], axis=1)
    for s in range(n_seq):
        win_ref[s, SUBLANES:SUBLANES + q, :] = raw[s * q:(s + 1) * q]
    conv = jnp.zeros((n_seq * q, A_CONV_DIM), F32)
    for k in range(A_CONV):
        off = SUBLANES - (A_CONV - 1) + k
        tap = jnp.concatenate([win_ref[s, off:off + q, :] for s in range(n_seq)], axis=0)
        conv = conv + tap * cw_ref[k:k + 1, :]
    xc = conv + cb_ref[...]
    return xc * jax.nn.sigmoid(xc)


def _ssd_tile(xc, z_ref, dt_ref, dtb_ref, alog_ref, dsk_ref, nw_ref, eh_ref, eht_ref,
              h_in_ref, h_out_ref, ya_ref, *, n_seq, q):
    tm = n_seq * q
    x = xc[:, :A_D_INNER]
    bm = xc[:, A_D_INNER:A_D_INNER + A_GROUPS * A_STATE].astype(BF16)
    cm = xc[:, A_D_INNER + A_GROUPS * A_STATE:].astype(BF16)
    xb = x.astype(BF16)

    dt = jax.nn.softplus(dt_ref[...] + dtb_ref[...])
    a = -jnp.exp(alog_ref[...])
    da = dt * a
    ii = lax.broadcasted_iota(jnp.int32, (tm, tm), 0)
    jj = lax.broadcasted_iota(jnp.int32, (tm, tm), 1)
    causal = jj <= ii
    if n_seq > 1:
        causal = causal & ((ii // q) == (jj // q))
    tri = jnp.where(causal, 1.0, 0.0).astype(F32)
    cum = jnp.dot(tri, da, precision=lax.Precision.HIGHEST, preferred_element_type=F32)
    cum_t = cum.T
    dt_t = dt.T
    last = jnp.concatenate(
        [jnp.broadcast_to(cum[s * q + q - 1:s * q + q, :], (q, LANES)) for s in range(n_seq)], axis=0)

    eh = eh_ref[...]

    def expand(v):
        hi, lo = _split_bf16(v)
        return _dot(hi, eh) + _dot(lo, eh)

    exp_cum = expand(jnp.exp(cum))
    w_state = expand(jnp.exp(last - cum) * dt)
    xw = (x * w_state).astype(BF16)
    lane = lax.broadcasted_iota(jnp.int32, (tm, LANES), 1)
    low_half = lane < A_HEAD_DIM

    decs = []
    for s in range(n_seq):
        col = s * q + q - 1
        lb = jnp.broadcast_to(jnp.exp(cum_t[:, col:col + 1]), (LANES, LANES))
        hi, lo = _split_bf16(lb)
        decs.append(_dot(eht_ref[...], hi) + _dot(eht_ref[...], lo))

    heads_per_group = A_HEADS // A_GROUPS
    gw = heads_per_group * A_HEAD_DIM
    y_parts = []
    for g in range(A_GROUPS):
        cg = cm[:, g * A_STATE:(g + 1) * A_STATE]
        bg = bm[:, g * A_STATE:(g + 1) * A_STATE]
        scores = _dot_nt(cg, bg)
        pair_out = []
        for pr in range(heads_per_group // 2):
            lane0 = g * gw + pr * LANES
            xp = xb[:, lane0:lane0 + LANES]
            acc = jnp.zeros((tm, LANES), F32)
            for half in range(2):
                e = g * heads_per_group + pr * 2 + half
                seg = cum[:, e:e + 1] - cum_t[e:e + 1, :]
                dec = jnp.exp(jnp.where(causal, seg, NEG_BIG))
                m = (scores * dec * dt_t[e:e + 1, :]).astype(BF16)
                xh = jnp.where(low_half if half == 0 else jnp.logical_not(low_half), xp, jnp.zeros_like(xp))
                acc = acc + _dot(m, xh)
            pair_out.append(acc)
        y_diag = jnp.concatenate(pair_out, axis=1)
        offs = []
        for s in range(n_seq):
            r0, r1 = s * q, (s + 1) * q
            h_old = h_in_ref[s, g * gw:(g + 1) * gw, :]
            offs.append(_dot_nt(cg[r0:r1], h_old.astype(BF16)))
            st = _dot_tn(xw[r0:r1, g * gw:(g + 1) * gw], bg[r0:r1])
            h_out_ref[s, g * gw:(g + 1) * gw, :] = h_old * decs[s][g * gw:(g + 1) * gw, :] + st
        y_off = jnp.concatenate(offs, axis=0) * exp_cum[:, g * gw:(g + 1) * gw]
        y_parts.append(y_diag + y_off)
    y = jnp.concatenate(y_parts, axis=1) + dsk_ref[...] * x

    z = z_ref[...]
    gt = y * (z * jax.nn.sigmoid(z))
    outs = []
    for g in range(A_GROUPS):
        gg = gt[:, g * gw:(g + 1) * gw]
        outs.append(gg * lax.rsqrt(jnp.mean(gg * gg, axis=-1, keepdims=True) + EPS))
    ya_ref[...] = (jnp.concatenate(outs, axis=1) * nw_ref[...]).astype(ya_ref.dtype)


def _mamba_prompt_kernel(z_ref, xs_ref, bc_ref, dt_ref, dtb_ref, alog_ref, dsk_ref, nw_ref, eh_ref, eht_ref,
                         ya_ref, h_ref):
    @pl.when(pl.program_id(1) == 0)
    def _():
        h_ref[...] = jnp.zeros(h_ref.shape, F32)

    for g in range(z_ref.shape[0]):
        xc = jnp.concatenate([xs_ref[g], bc_ref[g]], axis=1)
        h_g = h_ref.at[pl.ds(g, 1)]
        _ssd_tile(xc, z_ref.at[g], dt_ref.at[g], dtb_ref, alog_ref, dsk_ref, nw_ref, eh_ref, eht_ref,
                  h_g, h_g, ya_ref.at[g], n_seq=1, q=A_CHUNK)


def _mamba_sample_kernel(z_ref, xs_ref, bc_ref, dt_ref, cin_ref, hin_ref, cw_ref, cb_ref, dtb_ref, alog_ref,
                         dsk_ref, nw_ref, eh_ref, eht_ref, ya_ref, cout_ref, h_ref, win_ref, *, n_seq, q):
    lo = SUBLANES - (A_CONV - 1)
    for s in range(n_seq):
        win_ref[s, lo:SUBLANES, :] = cin_ref[s]
    xc = _conv_silu_window(xs_ref, bc_ref, cw_ref, cb_ref, win_ref, n_seq=n_seq, q=q)
    _ssd_tile(xc, z_ref, dt_ref, dtb_ref, alog_ref, dsk_ref, nw_ref, eh_ref, eht_ref,
              hin_ref, h_ref, ya_ref, n_seq=n_seq, q=q)
    for s in range(n_seq):
        cout_ref[s] = win_ref[s, SUBLANES + q - (A_CONV - 1):SUBLANES + q, :]


def _conv_weights_specs():
    return [_const_spec((A_CONV, A_CONV_DIM)), _const_spec((1, A_CONV_DIM))]


def _mamba_weights_specs():
    return [_const_spec((1, LANES)), _const_spec((1, LANES)), _const_spec((1, A_D_INNER)),
            _const_spec((1, A_D_INNER)), _const_spec((LANES, A_D_INNER)), _const_spec((A_D_INNER, LANES))]


def _mamba_prompt(proj3, wts, nb, seqlen):
    q = A_CHUNK
    nc = seqlen // q
    n_par = 4
    blk = lambda col_block: pl.BlockSpec((n_par, q, A_D_INNER), lambda b, c: (b, c, col_block))
    return pl.pallas_call(
        _mamba_prompt_kernel,
        out_shape=[jax.ShapeDtypeStruct((nb, seqlen, A_D_INNER), BF16),
                   jax.ShapeDtypeStruct((nb, A_D_INNER, A_STATE), F32)],
        grid=(nb // n_par, nc),
        in_specs=[blk(COL_Z // A_D_INNER), blk(COL_X // A_D_INNER), blk(COL_BC // A_D_INNER),
                  pl.BlockSpec((n_par, q, LANES), lambda b, c: (b, c, COL_DT // LANES))] + _mamba_weights_specs(),
        out_specs=[pl.BlockSpec((n_par, q, A_D_INNER), lambda b, c: (b, c, 0)),
                   pl.BlockSpec((n_par, A_D_INNER, A_STATE), lambda b, c: (b, 0, 0))],
        compiler_params=_params(2),
        name="mamba_prompt",
    )(proj3, proj3, proj3, proj3, *wts)


def _mamba_sample(proj, conv_state, ssm_state, wts, nb, seqlen):
    n_seq = 8
    q = seqlen
    tm = n_seq * q
    wide = A_D_INNER // LANES
    kern = functools.partial(_mamba_sample_kernel, n_seq=n_seq, q=q)
    return pl.pallas_call(
        kern,
        out_shape=[jax.ShapeDtypeStruct((nb * seqlen, A_D_INNER), BF16),
                   jax.ShapeDtypeStruct((nb, A_CONV - 1, A_CONV_DIM), F32),
                   jax.ShapeDtypeStruct((nb, A_D_INNER, A_STATE), F32)],
        grid=(nb // n_seq,),
        in_specs=[pl.BlockSpec((tm, A_D_INNER), lambda i: (i, 0)),
                  pl.BlockSpec((tm, A_D_INNER), lambda i: (i, 1)),
                  pl.BlockSpec((tm, A_D_INNER), lambda i: (i, 2)),
                  pl.BlockSpec((tm, LANES), lambda i: (i, 4 * wide)),
                  pl.BlockSpec((n_seq, A_CONV - 1, A_CONV_DIM), lambda i: (i, 0, 0)),
                  pl.BlockSpec((n_seq, A_D_INNER, A_STATE), lambda i: (i, 0, 0))]
                 + _conv_weights_specs() + _mamba_weights_specs(),
        out_specs=[pl.BlockSpec((tm, A_D_INNER), lambda i: (i, 0)),
                   pl.BlockSpec((n_seq, A_CONV - 1, A_CONV_DIM), lambda i: (i, 0, 0)),
                   pl.BlockSpec((n_seq, A_D_INNER, A_STATE), lambda i: (i, 0, 0))],
        scratch_shapes=[pltpu.VMEM((n_seq, SUBLANES + q, A_CONV_DIM), F32)],
        compiler_params=_params(1),
        name="mamba_sample",
    )(proj, proj, proj, proj, conv_state, ssm_state, *wts)


def _s5_param_kernel(lr_ref, li_ref, ls_ref, br_ref, bi_ref, are_ref, aim_ref, bbr_ref, bbi_ref):
    lr = lr_ref[...]
    li = li_ref[...]
    step = jnp.exp(ls_ref[...])
    mag = jnp.exp(lr * step)
    ab_re = mag * jnp.cos(li * step)
    ab_im = mag * jnp.sin(li * step)
    den = lr * lr + li * li
    k_re = ((ab_re - 1.0) * lr + ab_im * li) / den
    k_im = (ab_im * lr - (ab_re - 1.0) * li) / den
    br = br_ref[...]
    bi = bi_ref[...]
    are_ref[...] = ab_re
    aim_ref[...] = ab_im
    bbr_ref[...] = k_re * br - k_im * bi
    bbi_ref[...] = k_re * bi + k_im * br


def _s5_params(lam_re, lam_im, log_step, b_re, b_im):
    g, p, c = B_GROUPS, B_STATE, B_GROUP_CH
    return pl.pallas_call(
        _s5_param_kernel,
        out_shape=[jax.ShapeDtypeStruct((g, 1, p), F32), jax.ShapeDtypeStruct((g, 1, p), F32),
                   jax.ShapeDtypeStruct((g, c, p), F32), jax.ShapeDtypeStruct((g, c, p), F32)],
        name="s5_params",
    )(lam_re.reshape(g, 1, p), lam_im.reshape(g, 1, p), log_step.reshape(g, 1, 1),
      jnp.swapaxes(b_re, 1, 2), jnp.swapaxes(b_im, 1, 2))


def _block_diag(m, blk):
    g, r, c = m.shape
    eye = jnp.eye(blk, dtype=m.dtype)
    t = m.reshape(g // blk, blk, r, c)
    return (t[:, :, :, None, :] * eye[None, :, None, :, None]).reshape(g // blk, blk * r, blk * c)


def _gelu_tanh(x):
    return x * (0.5 * (1.0 + jnp.tanh(math.sqrt(2.0 / math.pi) * (x + 0.044715 * (x * x * x)))))


def _unit_row(sq, tc):
    u, b = divmod(sq, SUBLANES)
    return u * tc * SUBLANES + b


def _s5_kernel(*refs, n_unit, tc, has_state):
    if has_state:
        (u_ref, sre_ref, sim_ref, are_ref, aim_ref, bb_ref, cc_ref, dsk_ref, wg_ref, bg_ref,
         o_ref, hre_ref, him_ref, ubuf, buf, hst, obuf) = refs
    else:
        (u_ref, are_ref, aim_ref, bb_ref, cc_ref, dsk_ref, wg_ref, bg_ref,
         o_ref, hre_ref, him_ref, ubuf, buf, hst, obuf) = refs
    nb = SUBLANES * n_unit
    half = S5_SLABS

    for sq in range(nb):
        for i in range(N_SLAB):
            ubuf[i, pl.ds(_unit_row(sq, tc), tc, stride=SUBLANES), :] = u_ref[sq, :, i * LANES:(i + 1) * LANES]

    if has_state:
        for j in range(S5_SLABS):
            hst[j] = sre_ref[:, j * LANES:(j + 1) * LANES]
            hst[half + j] = sim_ref[:, j * LANES:(j + 1) * LANES]
    else:
        @pl.when(pl.program_id(0) == 0)
        def _():
            hst[...] = jnp.zeros(hst.shape, F32)

    per = S5_SLABS // N_SLAB

    def b_proj(i):
        r = _dot(ubuf[i].astype(BF16), bb_ref[i])
        for k in range(per):
            buf[per * i + k] = r[:, k * LANES:(k + 1) * LANES]
            buf[half + per * i + k] = r[:, (per + k) * LANES:(per + k + 1) * LANES]

    def scan(i):
        a_re = [jnp.broadcast_to(are_ref[per * i + k:per * i + k + 1, :], (SUBLANES, LANES)) for k in range(per)]
        a_im = [jnp.broadcast_to(aim_ref[per * i + k:per * i + k + 1, :], (SUBLANES, LANES)) for k in range(per)]
        for u in range(n_unit):
            rows = slice(u * SUBLANES, (u + 1) * SUBLANES)
            h_re = [hst[per * i + k, rows, :] for k in range(per)]
            h_im = [hst[half + per * i + k, rows, :] for k in range(per)]
            for t in range(tc):
                r0 = (u * tc + t) * SUBLANES
                for k in range(per):
                    j = per * i + k
                    n_re = a_re[k] * h_re[k] - a_im[k] * h_im[k] + buf[j, r0:r0 + SUBLANES, :]
                    n_im = a_re[k] * h_im[k] + a_im[k] * h_re[k] + buf[half + j, r0:r0 + SUBLANES, :]
                    buf[j, r0:r0 + SUBLANES, :] = n_re
                    buf[half + j, r0:r0 + SUBLANES, :] = n_im
                    h_re[k], h_im[k] = n_re, n_im
            for k in range(per):
                hst[per * i + k, rows, :] = h_re[k]
                hst[half + per * i + k, rows, :] = h_im[k]

    def c_proj(i):
        lhs = jnp.concatenate([buf[per * i + k] for k in range(per)]
                              + [buf[half + per * i + k] for k in range(per)], axis=1).astype(BF16)
        yi = _dot(lhs, cc_ref[i]) + dsk_ref[:, i * LANES:(i + 1) * LANES] * ubuf[i]
        return _gelu_tanh(yi)

    ys = []
    b_proj(0)
    for i in range(N_SLAB):
        if i + 1 < N_SLAB:
            b_proj(i + 1)
        scan(i)
        ys.append(c_proj(i))

    for j in range(S5_SLABS):
        hre_ref[:, j * LANES:(j + 1) * LANES] = hst[j]
        him_ref[:, j * LANES:(j + 1) * LANES] = hst[half + j]

    y = jnp.concatenate(ys, axis=1)
    out = y * jax.nn.sigmoid(_dot(y.astype(BF16), wg_ref[...]) + bg_ref[...])
    for i in range(N_SLAB):
        obuf[i] = out[:, i * LANES:(i + 1) * LANES]
    for sq in range(nb):
        for i in range(N_SLAB):
            o_ref[sq, :, i * LANES:(i + 1) * LANES] = \
                obuf[i, pl.ds(_unit_row(sq, tc), tc, stride=SUBLANES), :].astype(o_ref.dtype)


def _s5(proj3, state, wts, n_unit, tc):
    nb_all, seqlen, _ = proj3.shape
    nb = SUBLANES * n_unit
    rows = nb * tc
    has_state = state is not None
    if has_state:
        grid = (nb_all // nb,)
        u_map = lambda i: (i, 0, COL_U // B_CH)
        st_map = lambda i: (i, 0)
    else:
        grid = (seqlen // tc,)
        u_map = lambda i: (0, i, COL_U // B_CH)
        st_map = lambda i: (0, 0)
    st_spec = pl.BlockSpec((nb, B_NSTATE), st_map)
    in_specs = [pl.BlockSpec((nb, tc, B_CH), u_map)]
    args = [proj3]
    if has_state:
        in_specs += [st_spec, st_spec]
        args += list(state)
    in_specs += [_const_spec((S5_SLABS, LANES)), _const_spec((S5_SLABS, LANES)),
                 _const_spec((N_SLAB, LANES, 2 * S5_BLK * B_STATE)),
                 _const_spec((N_SLAB, 2 * S5_BLK * B_STATE, LANES)),
                 _const_spec((1, B_CH)), _const_spec((B_CH, B_CH)), _const_spec((1, B_CH))]
    args += list(wts)
    kern = functools.partial(_s5_kernel, n_unit=n_unit, tc=tc, has_state=has_state)
    return pl.pallas_call(
        kern,
        out_shape=[jax.ShapeDtypeStruct((nb_all, seqlen, B_CH), BF16 if tc % (2 * SUBLANES) == 0 else F32),
                   jax.ShapeDtypeStruct((nb_all, B_NSTATE), F32),
                   jax.ShapeDtypeStruct((nb_all, B_NSTATE), F32)],
        grid=grid,
        in_specs=in_specs,
        out_specs=[pl.BlockSpec((nb, tc, B_CH), (lambda i: (i, 0, 0)) if has_state else (lambda i: (0, i, 0))),
                   st_spec, st_spec],
        scratch_shapes=[pltpu.VMEM((N_SLAB, rows, LANES), F32),
                        pltpu.VMEM((2 * S5_SLABS, rows, LANES), F32),
                        pltpu.VMEM((2 * S5_SLABS, nb, LANES), F32),
                        pltpu.VMEM((N_SLAB, rows, LANES), F32)],
        compiler_params=_params(1),
        name="s5_state" if has_state else "s5_prompt",
    )(*args)


def _ff_kernel(*refs, has_mix, has_final):
    refs = list(refs)
    x_ref = refs.pop(0)
    if has_mix:
        ya_ref, yb_ref, wo_ref = refs.pop(0), refs.pop(0), refs.pop(0)
    nw_ref, w1_ref, w2_ref = refs.pop(0), refs.pop(0), refs.pop(0)
    if has_final:
        fw_ref = refs.pop(0)
    (o_ref,) = refs
    x = x_ref[...]
    if has_mix:
        mix = _dot(ya_ref[...].astype(BF16), wo_ref[0:A_D_INNER, :]) \
            + _dot(yb_ref[...].astype(BF16), wo_ref[A_D_INNER:A_D_INNER + B_CH, :])
        x = mix + x
    xn = _rms(x, nw_ref[...]).astype(BF16)
    acc = x
    step = 1024
    for c0 in range(0, D_FF, step):
        h = jnp.maximum(_dot(xn, w1_ref[:, c0:c0 + step]), 0.0)
        acc = acc + _dot((h * h).astype(BF16), w2_ref[c0:c0 + step, :])
    if has_final:
        acc = _rms(acc, fw_ref[...])
    o_ref[...] = acc


def _layer_spec(shape, layer):
    return pl.BlockSpec((None,) + tuple(shape), lambda *_: (layer, 0, 0), pipeline_mode=pl.Buffered(1))


def _ff(x2d, mix, nw, w1, w2, layer, final_w):
    m = x2d.shape[0]
    tm = 512
    row_spec = pl.BlockSpec((tm, D_MODEL), lambda i: (i, 0))
    in_specs, args = [row_spec], [x2d]
    if mix is not None:
        ya, yb, wo = mix
        in_specs += [row_spec, row_spec, _const_spec(wo.shape)]
        args += [ya, yb, wo]
    in_specs += [_const_spec((1, D_MODEL)), _layer_spec((D_MODEL, D_FF), layer), _layer_spec((D_FF, D_MODEL), layer)]
    args += [nw, w1, w2]
    if final_w is not None:
        in_specs.append(_const_spec((1, D_MODEL)))
        args.append(final_w)
    kern = functools.partial(_ff_kernel, has_mix=mix is not None, has_final=final_w is not None)
    return pl.pallas_call(
        kern,
        out_shape=jax.ShapeDtypeStruct((m, D_MODEL), F32),
        grid=(m // tm,),
        in_specs=in_specs,
        out_specs=row_spec,
        compiler_params=_params(1),
        name="ff_mix" if mix is not None else "ff_final",
    )(*args)


def _dw_conv_slab(wbuf, i, win0, out, oi, out0, n_out, wd_ref, bd_ref, dep=None):
    bias = bd_ref[i]
    per_pass = 16
    bounds = list(range(0, C_WIDTH, per_pass)) + [C_WIDTH]
    n_pass = len(bounds) - 1
    last = None
    for b in range(n_pass):
        k0, k1 = bounds[b], bounds[b + 1]
        taps = []
        for k in range(k0, k1):
            t = wd_ref[i, k:k + 1, :] if dep is None else wd_ref[i, k:k + 1, :] + dep
            taps.append(jnp.broadcast_to(t, (SUBLANES, LANES)))
        acc = {}
        for j in range(k0, n_out + k1 - 1):
            xj = wbuf[i, win0 + j * SUBLANES:win0 + (j + 1) * SUBLANES, :]
            for k in range(k0, k1):
                o = j - k
                if 0 <= o < n_out:
                    t = taps[k - k0] * xj
                    acc[o] = t if o not in acc else acc[o] + t
            o = j - (k1 - 1)
            if 0 <= o < n_out:
                v = acc.pop(o)
                rows = slice(out0 + o * SUBLANES, out0 + (o + 1) * SUBLANES)
                if b > 0:
                    v = out[oi, rows, :] + v
                if b == n_pass - 1:
                    v = v + bias
                out[oi, rows, :] = v
                last = v
        if dep is not None:
            dep = _zero_dep(last[0:1])
    return last


def _conf_kernel(*refs, n_unit, tc, has_state):
    if has_state:
        (x_ref, st_ref, nw_ref, w1_ref, b1_ref, wd_ref, bd_ref, lnw_ref, lnb_ref, w2_ref, b2_ref,
         o_ref, cst_ref, xbuf, wbuf, cvb, obuf) = refs
    else:
        (x_ref, nw_ref, w1_ref, b1_ref, wd_ref, bd_ref, lnw_ref, lnb_ref, w2_ref, b2_ref,
         o_ref, cst_ref, xbuf, wbuf, cvb, obuf) = refs
    nb = SUBLANES * n_unit
    win = (CONV_HIST + tc) * SUBLANES
    hist0 = (CONV_HIST - (C_WIDTH - 1)) * SUBLANES
    n_hist = C_WIDTH - 1
    tile = tc * SUBLANES

    for sq in range(nb):
        for i in range(N_SLAB):
            xbuf[i, pl.ds(_unit_row(sq, tc), tc, stride=SUBLANES), :] = x_ref[sq, :, i * LANES:(i + 1) * LANES]
    x = jnp.concatenate([xbuf[i] for i in range(N_SLAB)], axis=1)
    xn = _rms(x, nw_ref[...]).astype(BF16)

    if has_state:
        for u in range(n_unit):
            for i in range(N_SLAB):
                wbuf[i, u * win + hist0:u * win + hist0 + n_hist * SUBLANES, :] = \
                    st_ref[:, u * SUBLANES:(u + 1) * SUBLANES, i * LANES:(i + 1) * LANES].reshape(n_hist * SUBLANES, LANES)
    else:
        @pl.when(pl.program_id(0) == 0)
        def _():
            wbuf[:, 0:CONV_HIST * SUBLANES, :] = jnp.zeros((N_SLAB, CONV_HIST * SUBLANES, LANES), F32)

    chunk = 8 * SUBLANES
    pair = 2 * LANES
    for p in range(N_SLAB // 2):
        val = _dot(xn, w1_ref[:, p * pair:(p + 1) * pair]) + b1_ref[:, p * pair:(p + 1) * pair]
        gate = _dot(xn, w1_ref[:, C_CH + p * pair:C_CH + (p + 1) * pair]) + b1_ref[:, C_CH + p * pair:C_CH + (p + 1) * pair]
        h = val * jax.nn.sigmoid(gate)
        for i in (2 * p, 2 * p + 1):
            for u in range(n_unit):
                wbuf[i, u * win + CONV_HIST * SUBLANES:(u + 1) * win, :] = \
                    h[u * tile:(u + 1) * tile, (i - 2 * p) * LANES:(i - 2 * p + 1) * LANES]

    def conv_slab(i, carry):
        for u in range(n_unit):
            _dw_conv_slab(wbuf, i, u * win + hist0, cvb, i, u * tile, tc, wd_ref, bd_ref)
        return carry

    lax.fori_loop(0, N_SLAB, conv_slab, 0)

    for u in range(n_unit):
        tail = u * win + (CONV_HIST + tc - n_hist) * SUBLANES
        for i in range(N_SLAB):
            cst_ref[:, u * SUBLANES:(u + 1) * SUBLANES, i * LANES:(i + 1) * LANES] = \
                wbuf[i, tail:tail + n_hist * SUBLANES, :].reshape(n_hist, SUBLANES, LANES)
    if not has_state:
        wbuf[:, hist0:CONV_HIST * SUBLANES, :] = wbuf[:, hist0 + tile:CONV_HIST * SUBLANES + tile, :]

    c = jnp.concatenate([cvb[i] for i in range(N_SLAB)], axis=1)
    mu = jnp.mean(c, axis=-1, keepdims=True)
    cc = c - mu
    yn = cc * lax.rsqrt(jnp.mean(cc * cc, axis=-1, keepdims=True) + EPS) * lnw_ref[...] + lnb_ref[...]
    act = yn * jax.nn.sigmoid(yn)
    out = (_dot(act.astype(BF16), w2_ref[...]) + b2_ref[...]) + x
    for i in range(N_SLAB):
        obuf[i] = out[:, i * LANES:(i + 1) * LANES]
    for sq in range(nb):
        for i in range(N_SLAB):
            o_ref[sq, :, i * LANES:(i + 1) * LANES] = obuf[i, pl.ds(_unit_row(sq, tc), tc, stride=SUBLANES), :]


def _conformer(x3, state, wts, n_unit, tc):
    nb_all, seqlen, _ = x3.shape
    nb = SUBLANES * n_unit
    rows = nb * tc
    has_state = state is not None
    n_hist = C_WIDTH - 1
    if has_state:
        grid = (nb_all // nb,)
        x_map = lambda i: (i, 0, 0)
        st_map = lambda i: (0, i, 0)
    else:
        grid = (seqlen // tc,)
        x_map = lambda i: (0, i, 0)
        st_map = lambda i: (0, 0, 0)
    x_spec = pl.BlockSpec((nb, tc, C_CH), x_map)
    st_spec = pl.BlockSpec((n_hist, nb, C_CH), st_map)
    in_specs, args = [x_spec], [x3]
    if has_state:
        in_specs.append(st_spec)
        args.append(state)
    in_specs += [_const_spec((1, D_MODEL)), _const_spec((D_MODEL, 2 * C_CH)), _const_spec((1, 2 * C_CH)),
                 _const_spec((N_SLAB, C_WIDTH, LANES)), _const_spec((N_SLAB, 1, LANES)),
                 _const_spec((1, C_CH)), _const_spec((1, C_CH)),
                 _const_spec((C_CH, D_MODEL)), _const_spec((1, D_MODEL))]
    args += list(wts)
    kern = functools.partial(_conf_kernel, n_unit=n_unit, tc=tc, has_state=has_state)
    return pl.pallas_call(
        kern,
        out_shape=[jax.ShapeDtypeStruct((nb_all, seqlen, D_MODEL), F32),
                   jax.ShapeDtypeStruct((n_hist, nb_all, C_CH), F32)],
        grid=grid,
        in_specs=in_specs,
        out_specs=[x_spec, st_spec],
        scratch_shapes=[pltpu.VMEM((N_SLAB, rows, LANES), F32),
                        pltpu.VMEM((N_SLAB, n_unit * (CONV_HIST + tc) * SUBLANES, LANES), F32),
                        pltpu.VMEM((N_SLAB, rows, LANES), F32),
                        pltpu.VMEM((N_SLAB, rows, LANES), F32)],
        compiler_params=_params(1),
        name="conf_state" if has_state else "conf_prompt",
    )(*args)


def _glu_kernel(x_ref, nw_ref, w1_ref, b1_ref, h_ref):
    xn = _rms(x_ref[...], nw_ref[...]).astype(BF16)
    step = 512
    for c0 in range(0, C_CH, step):
        val = _dot(xn, w1_ref[:, c0:c0 + step]) + b1_ref[:, c0:c0 + step]
        gate = _dot(xn, w1_ref[:, C_CH + c0:C_CH + c0 + step]) + b1_ref[:, C_CH + c0:C_CH + c0 + step]
        h_ref[:, c0:c0 + step] = val * jax.nn.sigmoid(gate)


def _glu(x2d, nw, w1, b1):
    m = x2d.shape[0]
    tm = 512
    row_spec = pl.BlockSpec((tm, D_MODEL), lambda i: (i, 0))
    return pl.pallas_call(
        _glu_kernel,
        out_shape=jax.ShapeDtypeStruct((m, C_CH), F32),
        grid=(m // tm,),
        in_specs=[row_spec, _const_spec((1, D_MODEL)), _const_spec((D_MODEL, 2 * C_CH)), _const_spec((1, 2 * C_CH))],
        out_specs=row_spec,
        compiler_params=_params(1),
        name="conf_glu",
    )(x2d, nw, w1, b1)


def _zero_dep(v):
    return jnp.minimum(jnp.abs(v), 0.0)


def _conv_ff_kernel(h_ref, x_ref, wd_ref, bd_ref, lnw_ref, lnb_ref, wp_ref, bp_ref, nw_ref, w1_ref, w2_ref, fw_ref,
                    o_ref, cst_ref, wbuf, cvb, xbuf, xn_s, hs, act_s, *, tc):
    s = pl.program_id(0)
    tile = tc * SUBLANES
    hist0 = (CONV_HIST - (C_WIDTH - 1)) * SUBLANES
    n_hist = C_WIDTH - 1
    cur0 = CONV_HIST * SUBLANES

    @pl.when(s == 0)
    def _():
        wbuf[:, 0:cur0, :] = jnp.zeros((N_SLAB, cur0, LANES), F32)
        act_s[...] = jnp.zeros(act_s.shape, BF16)

    for sq in range(SUBLANES):
        for i in range(N_SLAB):
            xbuf[i, pl.ds(sq, tc, stride=SUBLANES), :] = x_ref[sq, :, i * LANES:(i + 1) * LANES]
            wbuf[i, pl.ds(cur0 + sq, tc, stride=SUBLANES), :] = h_ref[sq, :, i * LANES:(i + 1) * LANES]
    x = jnp.concatenate([xbuf[i] for i in range(N_SLAB)], axis=1)
    x3 = (_dot(act_s[1 - s % 2], wp_ref[...]) + bp_ref[...]) + x
    xn_s[...] = _rms(x3, nw_ref[...]).astype(BF16)

    jobs = list(range(N_SLAB))

    def conv_group(n, dep):
        last = None
        for _ in range(n):
            i = jobs.pop(0)
            last = _dw_conv_slab(wbuf, i, hist0, cvb, i, 0, tc, wd_ref, bd_ref, dep)
            dep = _zero_dep(last[0:1])
        return last

    def norm_act():
        c = jnp.concatenate([cvb[i] for i in range(N_SLAB)], axis=1)
        mu = jnp.mean(c, axis=-1, keepdims=True)
        cc = c - mu
        yn = cc * lax.rsqrt(jnp.mean(cc * cc, axis=-1, keepdims=True) + EPS) * lnw_ref[...] + lnb_ref[...]
        act_s[s % 2] = (yn * jax.nn.sigmoid(yn)).astype(BF16)

    step = 1024
    n_chunk = D_FF // step
    groups = [3, 3, 2, 0]
    acc = x3
    last = None
    up = _dot(xn_s[...], w1_ref[:, 0:step])
    for p in range(n_chunk):
        up_next = _dot(xn_s[...], w1_ref[:, (p + 1) * step:(p + 2) * step]) if p + 1 < n_chunk else None
        hmid = jnp.maximum(up, 0.0)
        hs[p % 2] = (hmid * hmid).astype(BF16)
        if last is not None:
            tie = jnp.concatenate([_zero_dep(last), _zero_dep(last)], axis=0).astype(BF16)
            hs[p % 2, 0:2 * SUBLANES, 0:LANES] = hs[p % 2, 0:2 * SUBLANES, 0:LANES] + tie
        part = _dot(hs[p % 2], w2_ref[p * step:(p + 1) * step, :])
        if groups[p]:
            last = conv_group(groups[p], _zero_dep(up[0:1, 0:LANES]))
        else:
            last = None
            norm_act()
        acc = acc + part
        up = up_next
    out = _rms(acc, fw_ref[...])

    for i in range(N_SLAB):
        cst_ref[:, :, i * LANES:(i + 1) * LANES] = \
            wbuf[i, (CONV_HIST + tc - n_hist) * SUBLANES:(CONV_HIST + tc) * SUBLANES, :].reshape(n_hist, SUBLANES, LANES)
    wbuf[:, hist0:cur0, :] = wbuf[:, hist0 + tile:cur0 + tile, :]

    for i in range(N_SLAB):
        xbuf[i] = out[:, i * LANES:(i + 1) * LANES]
    for sq in range(SUBLANES):
        for i in range(N_SLAB):
            o_ref[sq, :, i * LANES:(i + 1) * LANES] = xbuf[i, pl.ds(sq, tc, stride=SUBLANES), :]


def _conv_ff(h3, x3d, conf_w, nw, w1, w2, layer, final_w, tc):
    nb, seqlen, _ = h3.shape
    n_tiles = seqlen // tc
    n_hist = C_WIDTH - 1
    rows = nb * tc
    wd, bd, lnw, lnb, wp, bp = conf_w
    kern = functools.partial(_conv_ff_kernel, tc=tc)
    tile_spec = lambda f: pl.BlockSpec((nb, tc, C_CH), lambda s: (0, f(s), 0))
    prev = lambda s: jnp.maximum(s - 1, 0)
    return pl.pallas_call(
        kern,
        out_shape=[jax.ShapeDtypeStruct((nb, seqlen, D_MODEL), F32),
                   jax.ShapeDtypeStruct((n_hist, nb, C_CH), F32)],
        grid=(n_tiles + 1,),
        in_specs=[tile_spec(lambda s: jnp.minimum(s, n_tiles - 1)), tile_spec(prev),
                  _const_spec((C_WIDTH, C_CH)), _const_spec((1, C_CH)), _const_spec((1, C_CH)), _const_spec((1, C_CH)),
                  _const_spec((C_CH, D_MODEL)), _const_spec((1, D_MODEL)), _const_spec((1, D_MODEL)),
                  _layer_spec((D_MODEL, D_FF), layer), _layer_spec((D_FF, D_MODEL), layer), _const_spec((1, D_MODEL))],
        out_specs=[tile_spec(prev), pl.BlockSpec((n_hist, nb, C_CH), lambda s: (0, 0, 0))],
        scratch_shapes=[pltpu.VMEM((N_SLAB, (CONV_HIST + tc) * SUBLANES, LANES), F32),
                        pltpu.VMEM((N_SLAB, rows, LANES), F32),
                        pltpu.VMEM((N_SLAB, rows, LANES), F32),
                        pltpu.VMEM((rows, D_MODEL), BF16),
                        pltpu.VMEM((2, rows, D_FF // 4), BF16),
                        pltpu.VMEM((2, rows, D_MODEL), BF16)],
        compiler_params=_params(1),
        name="conv_ff",
    )(h3, x3d, wd, bd, lnw, lnb, wp, bp, nw, w1, w2, final_w)


def _trunk(x, states, w):
    nb, seqlen, _ = x.shape
    has_state = states is not None
    x2 = x.reshape(nb * seqlen, D_MODEL)
    if has_state:
        a_conv, a_ssm, b_re, b_im, c_conv = states
        proj = _inproj(x2, w["norm_mix0"], w["w_in"])
        ya, new_a_conv, new_a_ssm = _mamba_sample(proj, a_conv, a_ssm.reshape(nb, A_D_INNER, A_STATE),
                                                  w["a_conv"] + w["mamba"], nb, seqlen)
        yb, new_b_re, new_b_im = _s5(proj.reshape(nb, seqlen, PROJ_COLS),
                                     (b_re.reshape(nb, B_NSTATE), b_im.reshape(nb, B_NSTATE)), w["s5"],
                                     n_unit=8, tc=seqlen)
    else:
        proj, new_a_conv = _inproj_conv(x2, w["norm_mix0"], w["w_in"], *w["a_conv"], nb, seqlen)
        proj3 = proj.reshape(nb, seqlen, PROJ_COLS)
        ya, new_a_ssm = _mamba_prompt(proj3, w["mamba"], nb, seqlen)
        yb, new_b_re, new_b_im = _s5(proj3, None, w["s5"], n_unit=1, tc=64)
    x2 = _ff(x2, (ya.reshape(nb * seqlen, A_D_INNER), yb.reshape(nb * seqlen, B_CH), w["w_out"]), w["norm_ff0"], w["w_ff1"], w["w_ff2"], 0, None)
    if has_state:
        x3, new_c_conv = _conformer(x2.reshape(nb, seqlen, D_MODEL), c_conv, w["conf"], n_unit=4, tc=seqlen)
    else:
        x3, new_c_conv = _conformer(x2.reshape(nb, seqlen, D_MODEL), None, w["conf"], n_unit=1, tc=64)
    y = _ff(x3.reshape(nb * seqlen, D_MODEL), None, w["norm_ff1"], w["w_ff1"], w["w_ff2"], 1, w["norm_final"])
    return (y.reshape(nb, seqlen, D_MODEL),
            new_a_conv[None],
            new_a_ssm.reshape(1, nb, A_HEADS, A_HEAD_DIM, A_STATE),
            new_b_re.reshape(1, nb, B_GROUPS, B_STATE),
            new_b_im.reshape(1, nb, B_GROUPS, B_STATE),
            jnp.swapaxes(new_c_conv, 0, 1)[None])


def _prepare_weights(norm_mix, norm_ff, norm_final, w_in_ab, a_conv_w, a_conv_b, a_dt_bias, a_log, a_d, a_norm,
                     s5_lam_re, s5_lam_im, s5_log_step, s5_b_re, s5_b_im, s5_c_re, s5_c_im, s5_d, s5_w_glu,
                     s5_b_glu, w_out_ab, c_w_pw1, c_b_pw1, c_w_dw, c_b_dw, c_ln_w, c_ln_b, c_w_pw2, c_b_pw2,
                     w_ff1, w_ff2):
    row = lambda v: v.reshape(1, -1)
    a_proj = A_D_INNER + A_CONV_DIM + A_HEADS
    w_in = w_in_ab[0]
    dt_pad = LANES - A_HEADS
    w_in_r = jnp.concatenate([w_in[:, :A_D_INNER + A_CONV_DIM], w_in[:, a_proj:],
                              w_in[:, A_D_INNER + A_CONV_DIM:a_proj],
                              jnp.zeros((D_MODEL, dt_pad), F32)], axis=1).astype(BF16)
    pad_heads = lambda v: jnp.pad(v.reshape(1, A_HEADS), ((0, 0), (0, dt_pad)))
    head_lane = jnp.arange(A_D_INNER) // A_HEAD_DIM
    expand_heads = (jnp.arange(LANES)[:, None] == head_lane[None, :]).astype(BF16)
    a_conv = (a_conv_w[0], row(a_conv_b[0]))
    mamba = (pad_heads(a_dt_bias[0]), pad_heads(a_log[0]),
             row(jnp.repeat(a_d[0], A_HEAD_DIM)), row(a_norm[0]), expand_heads, expand_heads.T)

    ab_re, ab_im, bb_re, bb_im = _s5_params(s5_lam_re[0], s5_lam_im[0], s5_log_step[0], s5_b_re[0], s5_b_im[0])
    bb = jnp.concatenate([_block_diag(bb_re, S5_BLK), _block_diag(bb_im, S5_BLK)], axis=2).astype(BF16)
    c_re_t = jnp.swapaxes(s5_c_re[0], 1, 2)
    c_im_t = jnp.swapaxes(s5_c_im[0], 1, 2)
    cc = jnp.concatenate([_block_diag(c_re_t, S5_BLK), _block_diag(-c_im_t, S5_BLK)], axis=1).astype(BF16)
    s5 = (ab_re.reshape(S5_SLABS, LANES), ab_im.reshape(S5_SLABS, LANES), bb, cc, row(s5_d[0]),
          s5_w_glu[0].astype(BF16), row(s5_b_glu[0]))

    slab_major = lambda v: jnp.swapaxes(v.reshape(-1, N_SLAB, LANES), 0, 1)
    conf = (row(norm_mix[1]), c_w_pw1[0].astype(BF16), row(c_b_pw1[0]), slab_major(c_w_dw[0]), slab_major(c_b_dw[0]),
            row(c_ln_w[0]), row(c_ln_b[0]), c_w_pw2[0].astype(BF16), row(c_b_pw2[0]))
    return dict(norm_mix0=row(norm_mix[0]), w_in=w_in_r, a_conv=a_conv, mamba=mamba, s5=s5, w_out=w_out_ab[0].astype(BF16),
                norm_ff0=row(norm_ff[0]), norm_ff1=row(norm_ff[1]), w_ff1=w_ff1.astype(BF16), w_ff2=w_ff2.astype(BF16),
                conf=conf, norm_final=row(norm_final))


def kernel(x_prompt, x_sample, state_a_conv, state_a_ssm, state_b_re, state_b_im, state_c_conv, norm_mix, norm_ff, norm_final, w_in_ab, a_conv_w, a_conv_b, a_dt_bias, a_log, a_d, a_norm, s5_lam_re, s5_lam_im, s5_log_step, s5_b_re, s5_b_im, s5_c_re, s5_c_im, s5_d, s5_w_glu, s5_b_glu, w_out_ab, c_w_pw1, c_b_pw1, c_w_dw, c_b_dw, c_ln_w, c_ln_b, c_w_pw2, c_b_pw2, w_ff1, w_ff2):
    w = _prepare_weights(norm_mix, norm_ff, norm_final, w_in_ab, a_conv_w, a_conv_b, a_dt_bias, a_log, a_d, a_norm,
                         s5_lam_re, s5_lam_im, s5_log_step, s5_b_re, s5_b_im, s5_c_re, s5_c_im, s5_d, s5_w_glu,
                         s5_b_glu, w_out_ab, c_w_pw1, c_b_pw1, c_w_dw, c_b_dw, c_ln_w, c_ln_b, c_w_pw2, c_b_pw2,
                         w_ff1, w_ff2)
    prompt = _trunk(x_prompt, None, w)
    sample = _trunk(x_sample, (state_a_conv[0], state_a_ssm[0], state_b_re[0], state_b_im[0],
                               jnp.swapaxes(state_c_conv[0], 0, 1)), w)
    return (prompt[0], sample[0]) + prompt[1:] + sample[1:]
```

```python
import functools
import math

import jax
import jax.numpy as jnp
from jax import lax
from jax.experimental import pallas as pl
from jax.experimental.pallas import tpu as pltpu

F32 = jnp.float32
BF16 = jnp.bfloat16

D_MODEL = 1024
A_HEADS = 16
A_HEAD_DIM = 64
A_D_INNER = A_HEADS * A_HEAD_DIM
A_GROUPS = 4
A_STATE = 128
A_CONV = 4
A_CONV_DIM = A_D_INNER + 2 * A_GROUPS * A_STATE
A_CHUNK = 128
B_CH = 1024
B_GROUP_CH = 16
B_GROUPS = B_CH // B_GROUP_CH
B_STATE = 64
B_NSTATE = B_GROUPS * B_STATE
C_CH = D_MODEL
C_WIDTH = 31
D_FF = 4 * D_MODEL
EPS = 1e-6

SUBLANES = 8
LANES = 128
VMEM_LIMIT_BYTES = 56 * 1024 * 1024

COL_Z = 0
COL_X = A_D_INNER
COL_BC = 2 * A_D_INNER
COL_U = 3 * A_D_INNER
COL_DT = 4 * A_D_INNER
PROJ_COLS = COL_DT + LANES

NEG_BIG = -1e30
N_SLAB = B_CH // LANES
S5_SLABS = B_NSTATE // LANES
S5_BLK = 8
ROW_TILE = 512
ROW_TILE_FINAL = 1024
SSD_SEQS_PROMPT = 8
SSD_SEQS_SAMPLE = 8
TIME_TILE = 64
S5_UNITS_SAMPLE = 8
CONF_UNITS_SAMPLE = 4
CONV_HIST = 32
CONV_TAPS_PER_PASS = 16


def _params(n_grid):
    return pltpu.CompilerParams(dimension_semantics=("arbitrary",) * n_grid,
                                vmem_limit_bytes=VMEM_LIMIT_BYTES)


def _const_spec(shape):
    nd = len(shape)
    return pl.BlockSpec(shape, lambda *_: (0,) * nd, pipeline_mode=pl.Buffered(1))


def _rms(x, w):
    return x * lax.rsqrt(jnp.mean(x * x, axis=-1, keepdims=True) + EPS) * w


def _dot(a, b):
    return jnp.dot(a, b, preferred_element_type=F32)


def _dot_nt(a, b):
    return lax.dot_general(a, b, (((1,), (1,)), ((), ())), preferred_element_type=F32)


def _dot_tn(a, b):
    return lax.dot_general(a, b, (((0,), (0,)), ((), ())), preferred_element_type=F32)


def _split_bf16(v):
    hi = v.astype(BF16)
    lo = (v - hi.astype(F32)).astype(BF16)
    return hi, lo


def _inproj_kernel(x_ref, nw_ref, w_ref, o_ref):
    xn = _rms(x_ref[...], nw_ref[...]).astype(BF16)
    n = o_ref.shape[1]
    step = 512
    for c0 in range(0, n, step):
        c1 = min(c0 + step, n)
        o_ref[:, c0:c1] = _dot(xn, w_ref[:, c0:c1])


def _inproj(x2d, nw, w):
    m = x2d.shape[0]
    tm = ROW_TILE
    return pl.pallas_call(
        _inproj_kernel,
        out_shape=jax.ShapeDtypeStruct((m, PROJ_COLS), F32),
        grid=(m // tm,),
        in_specs=[pl.BlockSpec((tm, D_MODEL), lambda i: (i, 0)),
                  _const_spec((1, D_MODEL)),
                  _const_spec((D_MODEL, PROJ_COLS))],
        out_specs=pl.BlockSpec((tm, PROJ_COLS), lambda i: (i, 0)),
        compiler_params=_params(1),
        name="inproj",
    )(x2d, nw, w)


def _inproj_conv_kernel(x_ref, nw_ref, w_ref, cw_ref, cb_ref, o_ref, cst_ref, win_ref, *, tiles_per_seq):
    tm = x_ref.shape[0]
    lo = SUBLANES - (A_CONV - 1)

    @pl.when(pl.program_id(0) % tiles_per_seq == 0)
    def _():
        win_ref[...] = jnp.zeros(win_ref.shape, F32)

    xn = _rms(x_ref[...], nw_ref[...]).astype(BF16)
    step = 512
    plain = list(range(COL_Z, COL_X, step)) + list(range(COL_U, PROJ_COLS, step))

    def plain_chunk():
        c0 = plain.pop(0)
        c1 = min(c0 + step, PROJ_COLS)
        o_ref[:, c0:c1] = _dot(xn, w_ref[:, c0:c1])

    for c0 in range(COL_X, COL_U, step):
        w0 = c0 - COL_X
        raw = _dot(xn, w_ref[:, c0:c0 + step])
        plain_chunk()
        full = jnp.concatenate([win_ref[:, w0:w0 + step], raw], axis=0)
        conv = raw * cw_ref[A_CONV - 1:A_CONV, w0:w0 + step]
        for k in range(A_CONV - 1):
            conv = conv + full[lo + k:lo + k + tm] * cw_ref[k:k + 1, w0:w0 + step]
        xc = conv + cb_ref[:, w0:w0 + step]
        o_ref[:, c0:c0 + step] = xc * jax.nn.sigmoid(xc)
        win_ref[:, w0:w0 + step] = raw[tm - SUBLANES:tm]
    cst_ref[0] = win_ref[lo:SUBLANES, :]
    while plain:
        plain_chunk()


def _inproj_conv(x2d, nw, w, cw, cb, nb, seqlen):
    tm = ROW_TILE
    tiles_per_seq = seqlen // tm
    kern = functools.partial(_inproj_conv_kernel, tiles_per_seq=tiles_per_seq)
    return pl.pallas_call(
        kern,
        out_shape=[jax.ShapeDtypeStruct((nb * seqlen, PROJ_COLS), F32),
                   jax.ShapeDtypeStruct((nb, A_CONV - 1, A_CONV_DIM), F32)],
        grid=(nb * tiles_per_seq,),
        in_specs=[pl.BlockSpec((tm, D_MODEL), lambda i: (i, 0)),
                  _const_spec((1, D_MODEL)),
                  _const_spec((D_MODEL, PROJ_COLS)),
                  _const_spec((A_CONV, A_CONV_DIM)),
                  _const_spec((1, A_CONV_DIM))],
        out_specs=[pl.BlockSpec((tm, PROJ_COLS), lambda i: (i, 0)),
                   pl.BlockSpec((1, A_CONV - 1, A_CONV_DIM), lambda i: (i // tiles_per_seq, 0, 0))],
        scratch_shapes=[pltpu.VMEM((SUBLANES, A_CONV_DIM), F32)],
        compiler_params=_params(1),
        name="inproj_conv",
    )(x2d, nw, w, cw, cb)


def _conv_silu_window(xs_ref, bc_ref, cw_ref, cb_ref, win_ref, *, n_seq, q):
    raw = jnp.concatenate([xs_ref[...], bc_ref[...]], axis=1)
    for s in range(n_seq):
        win_ref[s, SUBLANES:SUBLANES + q, :] = raw[s * q:(s + 1) * q]
    conv = jnp.zeros((n_seq * q, A_CONV_DIM), F32)
    for k in range(A_CONV):
        off = SUBLANES - (A_CONV - 1) + k
        tap = jnp.concatenate([win_ref[s, off:off + q, :] for s in range(n_seq)], axis=0)
        conv = conv + tap * cw_ref[k:k + 1, :]
    xc = conv + cb_ref[...]
    return xc * jax.nn.sigmoid(xc)


def _ssd_tile(xc, z_ref, dt_ref, dtb_ref, alog_ref, dsk_ref, nw_ref, eh_ref, eht_ref,
              h_in_ref, h_out_ref, ya_ref, *, n_seq, q):
    tm = n_seq * q
    x = xc[:, :A_D_INNER]
    bm = xc[:, A_D_INNER:A_D_INNER + A_GROUPS * A_STATE].astype(BF16)
    cm = xc[:, A_D_INNER + A_GROUPS * A_STATE:].astype(BF16)
    xb = x.astype(BF16)

    dt = jax.nn.softplus(dt_ref[...] + dtb_ref[...])
    a = -jnp.exp(alog_ref[...])
    da = dt * a
    ii = lax.broadcasted_iota(jnp.int32, (tm, tm), 0)
    jj = lax.broadcasted_iota(jnp.int32, (tm, tm), 1)
    causal = jj <= ii
    if n_seq > 1:
        causal = causal & ((ii // q) == (jj // q))
    tri = jnp.where(causal, 1.0, 0.0).astype(F32)
    cum = jnp.dot(tri, da, precision=lax.Precision.HIGHEST, preferred_element_type=F32)
    cum_t = cum.T
    dt_t = dt.T
    last = jnp.concatenate(
        [jnp.broadcast_to(cum[s * q + q - 1:s * q + q, :], (q, LANES)) for s in range(n_seq)], axis=0)

    eh = eh_ref[...]

    def expand(v):
        hi, lo = _split_bf16(v)
        return _dot(hi, eh) + _dot(lo, eh)

    exp_cum = expand(jnp.exp(cum))
    w_state = expand(jnp.exp(last - cum) * dt)
    xw = (x * w_state).astype(BF16)
    lane = lax.broadcasted_iota(jnp.int32, (tm, LANES), 1)
    low_half = lane < A_HEAD_DIM

    decs = []
    for s in range(n_seq):
        col = s * q + q - 1
        lb = jnp.broadcast_to(jnp.exp(cum_t[:, col:col + 1]), (LANES, LANES))
        hi, lo = _split_bf16(lb)
        decs.append(_dot(eht_ref[...], hi) + _dot(eht_ref[...], lo))

    heads_per_group = A_HEADS // A_GROUPS
    gw = heads_per_group * A_HEAD_DIM
    y_parts = []
    for g in range(A_GROUPS):
        cg = cm[:, g * A_STATE:(g + 1) * A_STATE]
        bg = bm[:, g * A_STATE:(g + 1) * A_STATE]
        scores = _dot_nt(cg, bg)
        pair_out = []
        for pr in range(heads_per_group // 2):
            lane0 = g * gw + pr * LANES
            xp = xb[:, lane0:lane0 + LANES]
            acc = jnp.zeros((tm, LANES), F32)
            for half in range(2):
                e = g * heads_per_group + pr * 2 + half
                seg = cum[:, e:e + 1] - cum_t[e:e + 1, :]
                dec = jnp.exp(jnp.where(causal, seg, NEG_BIG))
                m = (scores * dec * dt_t[e:e + 1, :]).astype(BF16)
                xh = jnp.where(low_half if half == 0 else jnp.logical_not(low_half), xp, jnp.zeros_like(xp))
                acc = acc + _dot(m, xh)
            pair_out.append(acc)
        y_diag = jnp.concatenate(pair_out, axis=1)
        offs = []
        for s in range(n_seq):
            r0, r1 = s * q, (s + 1) * q
            h_old = h_in_ref[s, g * gw:(g + 1) * gw, :]
            offs.append(_dot_nt(cg[r0:r1], h_old.astype(BF16)))
            st = _dot_tn(xw[r0:r1, g * gw:(g + 1) * gw], bg[r0:r1])
            h_out_ref[s, g * gw:(g + 1) * gw, :] = h_old * decs[s][g * gw:(g + 1) * gw, :] + st
        y_off = jnp.concatenate(offs, axis=0) * exp_cum[:, g * gw:(g + 1) * gw]
        y_parts.append(y_diag + y_off)
    y = jnp.concatenate(y_parts, axis=1) + dsk_ref[...] * x

    z = z_ref[...]
    gt = y * (z * jax.nn.sigmoid(z))
    outs = []
    for g in range(A_GROUPS):
        gg = gt[:, g * gw:(g + 1) * gw]
        outs.append(gg * lax.rsqrt(jnp.mean(gg * gg, axis=-1, keepdims=True) + EPS))
    ya_ref[...] = (jnp.concatenate(outs, axis=1) * nw_ref[...]).astype(ya_ref.dtype)


def _mamba_prompt_kernel(z_ref, xs_ref, bc_ref, dt_ref, dtb_ref, alog_ref, dsk_ref, nw_ref, eh_ref, eht_ref,
                         ya_ref, h_ref):
    @pl.when(pl.program_id(1) == 0)
    def _():
        h_ref[...] = jnp.zeros(h_ref.shape, F32)

    for g in range(z_ref.shape[0]):
        xc = jnp.concatenate([xs_ref[g], bc_ref[g]], axis=1)
        h_g = h_ref.at[pl.ds(g, 1)]
        _ssd_tile(xc, z_ref.at[g], dt_ref.at[g], dtb_ref, alog_ref, dsk_ref, nw_ref, eh_ref, eht_ref,
                  h_g, h_g, ya_ref.at[g], n_seq=1, q=A_CHUNK)


def _mamba_sample_kernel(z_ref, xs_ref, bc_ref, dt_ref, cin_ref, hin_ref, cw_ref, cb_ref, dtb_ref, alog_ref,
                         dsk_ref, nw_ref, eh_ref, eht_ref, ya_ref, cout_ref, h_ref, win_ref, *, n_seq, q):
    lo = SUBLANES - (A_CONV - 1)
    for s in range(n_seq):
        win_ref[s, lo:SUBLANES, :] = cin_ref[s]
    xc = _conv_silu_window(xs_ref, bc_ref, cw_ref, cb_ref, win_ref, n_seq=n_seq, q=q)
    _ssd_tile(xc, z_ref, dt_ref, dtb_ref, alog_ref, dsk_ref, nw_ref, eh_ref, eht_ref,
              hin_ref, h_ref, ya_ref, n_seq=n_seq, q=q)
    for s in range(n_seq):
        cout_ref[s] = win_ref[s, SUBLANES + q - (A_CONV - 1):SUBLANES + q, :]


def _conv_weights_specs():
    return [_const_spec((A_CONV, A_CONV_DIM)), _const_spec((1, A_CONV_DIM))]


def _mamba_weights_specs():
    return [_const_spec((1, LANES)), _const_spec((1, LANES)), _const_spec((1, A_D_INNER)),
            _const_spec((1, A_D_INNER)), _const_spec((LANES, A_D_INNER)), _const_spec((A_D_INNER, LANES))]


def _mamba_prompt(proj3, wts, nb, seqlen):
    q = A_CHUNK
    nc = seqlen // q
    n_par = SSD_SEQS_PROMPT
    blk = lambda col_block: pl.BlockSpec((n_par, q, A_D_INNER), lambda b, c: (b, c, col_block))
    return pl.pallas_call(
        _mamba_prompt_kernel,
        out_shape=[jax.ShapeDtypeStruct((nb, seqlen, A_D_INNER), BF16),
                   jax.ShapeDtypeStruct((nb, A_D_INNER, A_STATE), F32)],
        grid=(nb // n_par, nc),
        in_specs=[blk(COL_Z // A_D_INNER), blk(COL_X // A_D_INNER), blk(COL_BC // A_D_INNER),
                  pl.BlockSpec((n_par, q, LANES), lambda b, c: (b, c, COL_DT // LANES))] + _mamba_weights_specs(),
        out_specs=[pl.BlockSpec((n_par, q, A_D_INNER), lambda b, c: (b, c, 0)),
                   pl.BlockSpec((n_par, A_D_INNER, A_STATE), lambda b, c: (b, 0, 0))],
        compiler_params=_params(2),
        name="mamba_prompt",
    )(proj3, proj3, proj3, proj3, *wts)


def _mamba_sample(proj, conv_state, ssm_state, wts, nb, seqlen):
    n_seq = SSD_SEQS_SAMPLE
    q = seqlen
    tm = n_seq * q
    wide = A_D_INNER // LANES
    kern = functools.partial(_mamba_sample_kernel, n_seq=n_seq, q=q)
    return pl.pallas_call(
        kern,
        out_shape=[jax.ShapeDtypeStruct((nb * seqlen, A_D_INNER), BF16),
                   jax.ShapeDtypeStruct((nb, A_CONV - 1, A_CONV_DIM), F32),
                   jax.ShapeDtypeStruct((nb, A_D_INNER, A_STATE), F32)],
        grid=(nb // n_seq,),
        in_specs=[pl.BlockSpec((tm, A_D_INNER), lambda i: (i, 0)),
                  pl.BlockSpec((tm, A_D_INNER), lambda i: (i, 1)),
                  pl.BlockSpec((tm, A_D_INNER), lambda i: (i, 2)),
                  pl.BlockSpec((tm, LANES), lambda i: (i, 4 * wide)),
                  pl.BlockSpec((n_seq, A_CONV - 1, A_CONV_DIM), lambda i: (i, 0, 0)),
                  pl.BlockSpec((n_seq, A_D_INNER, A_STATE), lambda i: (i, 0, 0))]
                 + _conv_weights_specs() + _mamba_weights_specs(),
        out_specs=[pl.BlockSpec((tm, A_D_INNER), lambda i: (i, 0)),
                   pl.BlockSpec((n_seq, A_CONV - 1, A_CONV_DIM), lambda i: (i, 0, 0)),
                   pl.BlockSpec((n_seq, A_D_INNER, A_STATE), lambda i: (i, 0, 0))],
        scratch_shapes=[pltpu.VMEM((n_seq, SUBLANES + q, A_CONV_DIM), F32)],
        compiler_params=_params(1),
        name="mamba_sample",
    )(proj, proj, proj, proj, conv_state, ssm_state, *wts)


def _s5_param_kernel(lr_ref, li_ref, ls_ref, br_ref, bi_ref, are_ref, aim_ref, bbr_ref, bbi_ref):
    lr = lr_ref[...]
    li = li_ref[...]
    step = jnp.exp(ls_ref[...])
    mag = jnp.exp(lr * step)
    ab_re = mag * jnp.cos(li * step)
    ab_im = mag * jnp.sin(li * step)
    den = lr * lr + li * li
    k_re = ((ab_re - 1.0) * lr + ab_im * li) / den
    k_im = (ab_im * lr - (ab_re - 1.0) * li) / den
    br = br_ref[...]
    bi = bi_ref[...]
    are_ref[...] = ab_re
    aim_ref[...] = ab_im
    bbr_ref[...] = k_re * br - k_im * bi
    bbi_ref[...] = k_re * bi + k_im * br


def _s5_params(lam_re, lam_im, log_step, b_re, b_im):
    g, p, c = B_GROUPS, B_STATE, B_GROUP_CH
    return pl.pallas_call(
        _s5_param_kernel,
        out_shape=[jax.ShapeDtypeStruct((g, 1, p), F32), jax.ShapeDtypeStruct((g, 1, p), F32),
                   jax.ShapeDtypeStruct((g, c, p), F32), jax.ShapeDtypeStruct((g, c, p), F32)],
        name="s5_params",
    )(lam_re.reshape(g, 1, p), lam_im.reshape(g, 1, p), log_step.reshape(g, 1, 1),
      jnp.swapaxes(b_re, 1, 2), jnp.swapaxes(b_im, 1, 2))


def _block_diag(m, blk):
    g, r, c = m.shape
    eye = jnp.eye(blk, dtype=m.dtype)
    t = m.reshape(g // blk, blk, r, c)
    return (t[:, :, :, None, :] * eye[None, :, None, :, None]).reshape(g // blk, blk * r, blk * c)


def _gelu_tanh(x):
    return x * (0.5 * (1.0 + jnp.tanh(math.sqrt(2.0 / math.pi) * (x + 0.044715 * (x * x * x)))))


def _unit_row(sq, tc):
    u, b = divmod(sq, SUBLANES)
    return u * tc * SUBLANES + b


def _s5_kernel(*refs, n_unit, tc, has_state):
    if has_state:
        (u_ref, sre_ref, sim_ref, are_ref, aim_ref, bb_ref, cc_ref, dsk_ref, wg_ref, bg_ref,
         o_ref, hre_ref, him_ref, ubuf, buf, hst, obuf) = refs
    else:
        (u_ref, are_ref, aim_ref, bb_ref, cc_ref, dsk_ref, wg_ref, bg_ref,
         o_ref, hre_ref, him_ref, ubuf, buf, hst, obuf) = refs
    nb = SUBLANES * n_unit
    half = S5_SLABS

    for sq in range(nb):
        for i in range(N_SLAB):
            ubuf[i, pl.ds(_unit_row(sq, tc), tc, stride=SUBLANES), :] = u_ref[sq, :, i * LANES:(i + 1) * LANES]

    if has_state:
        for j in range(S5_SLABS):
            hst[j] = sre_ref[:, j * LANES:(j + 1) * LANES]
            hst[half + j] = sim_ref[:, j * LANES:(j + 1) * LANES]
    else:
        @pl.when(pl.program_id(0) == 0)
        def _():
            hst[...] = jnp.zeros(hst.shape, F32)

    per = S5_SLABS // N_SLAB

    def b_proj(i):
        r = _dot(ubuf[i].astype(BF16), bb_ref[i])
        for k in range(per):
            buf[per * i + k] = r[:, k * LANES:(k + 1) * LANES]
            buf[half + per * i + k] = r[:, (per + k) * LANES:(per + k + 1) * LANES]

    def scan(i):
        a_re = [jnp.broadcast_to(are_ref[per * i + k:per * i + k + 1, :], (SUBLANES, LANES)) for k in range(per)]
        a_im = [jnp.broadcast_to(aim_ref[per * i + k:per * i + k + 1, :], (SUBLANES, LANES)) for k in range(per)]
        for u in range(n_unit):
            rows = slice(u * SUBLANES, (u + 1) * SUBLANES)
            h_re = [hst[per * i + k, rows, :] for k in range(per)]
            h_im = [hst[half + per * i + k, rows, :] for k in range(per)]
            for t in range(tc):
                r0 = (u * tc + t) * SUBLANES
                for k in range(per):
                    j = per * i + k
                    n_re = a_re[k] * h_re[k] - a_im[k] * h_im[k] + buf[j, r0:r0 + SUBLANES, :]
                    n_im = a_re[k] * h_im[k] + a_im[k] * h_re[k] + buf[half + j, r0:r0 + SUBLANES, :]
                    buf[j, r0:r0 + SUBLANES, :] = n_re
                    buf[half + j, r0:r0 + SUBLANES, :] = n_im
                    h_re[k], h_im[k] = n_re, n_im
            for k in range(per):
                hst[per * i + k, rows, :] = h_re[k]
                hst[half + per * i + k, rows, :] = h_im[k]

    def c_proj(i):
        lhs = jnp.concatenate([buf[per * i + k] for k in range(per)]
                              + [buf[half + per * i + k] for k in range(per)], axis=1).astype(BF16)
        yi = _dot(lhs, cc_ref[i]) + dsk_ref[:, i * LANES:(i + 1) * LANES] * ubuf[i]
        return _gelu_tanh(yi)

    ys = []
    b_proj(0)
    for i in range(N_SLAB):
        if i + 1 < N_SLAB:
            b_proj(i + 1)
        scan(i)
        ys.append(c_proj(i))

    for j in range(S5_SLABS):
        hre_ref[:, j * LANES:(j + 1) * LANES] = hst[j]
        him_ref[:, j * LANES:(j + 1) * LANES] = hst[half + j]

    y_bf = jnp.concatenate(ys, axis=1).astype(BF16)
    pair = 2 * LANES
    for c in range(N_SLAB // 2):
        gate = jax.nn.sigmoid(_dot(y_bf, wg_ref[:, c * pair:(c + 1) * pair]) + bg_ref[:, c * pair:(c + 1) * pair])
        for k in range(2):
            obuf[2 * c + k] = ys[2 * c + k] * gate[:, k * LANES:(k + 1) * LANES]
    for sq in range(nb):
        for i in range(N_SLAB):
            o_ref[sq, :, i * LANES:(i + 1) * LANES] = \
                obuf[i, pl.ds(_unit_row(sq, tc), tc, stride=SUBLANES), :].astype(o_ref.dtype)


def _s5(proj3, state, wts, n_unit, tc):
    nb_all, seqlen, _ = proj3.shape
    nb = SUBLANES * n_unit
    rows = nb * tc
    has_state = state is not None
    if has_state:
        grid = (nb_all // nb,)
        u_map = lambda i: (i, 0, COL_U // B_CH)
        st_map = lambda i: (i, 0)
    else:
        grid = (seqlen // tc,)
        u_map = lambda i: (0, i, COL_U // B_CH)
        st_map = lambda i: (0, 0)
    st_spec = pl.BlockSpec((nb, B_NSTATE), st_map)
    in_specs = [pl.BlockSpec((nb, tc, B_CH), u_map)]
    args = [proj3]
    if has_state:
        in_specs += [st_spec, st_spec]
        args += list(state)
    in_specs += [_const_spec((S5_SLABS, LANES)), _const_spec((S5_SLABS, LANES)),
                 _const_spec((N_SLAB, LANES, 2 * S5_BLK * B_STATE)),
                 _const_spec((N_SLAB, 2 * S5_BLK * B_STATE, LANES)),
                 _const_spec((1, B_CH)), _const_spec((B_CH, B_CH)), _const_spec((1, B_CH))]
    args += list(wts)
    kern = functools.partial(_s5_kernel, n_unit=n_unit, tc=tc, has_state=has_state)
    return pl.pallas_call(
        kern,
        out_shape=[jax.ShapeDtypeStruct((nb_all, seqlen, B_CH), BF16 if tc % (2 * SUBLANES) == 0 else F32),
                   jax.ShapeDtypeStruct((nb_all, B_NSTATE), F32),
                   jax.ShapeDtypeStruct((nb_all, B_NSTATE), F32)],
        grid=grid,
        in_specs=in_specs,
        out_specs=[pl.BlockSpec((nb, tc, B_CH), (lambda i: (i, 0, 0)) if has_state else (lambda i: (0, i, 0))),
                   st_spec, st_spec],
        scratch_shapes=[pltpu.VMEM((N_SLAB, rows, LANES), F32),
                        pltpu.VMEM((2 * S5_SLABS, rows, LANES), F32),
                        pltpu.VMEM((2 * S5_SLABS, nb, LANES), F32),
                        pltpu.VMEM((N_SLAB, rows, LANES), F32)],
        compiler_params=_params(1),
        name="s5_state" if has_state else "s5_prompt",
    )(*args)


def _ff_kernel(*refs, has_mix, has_final):
    refs = list(refs)
    x_ref = refs.pop(0)
    if has_mix:
        ya_ref, yb_ref, wo_ref = refs.pop(0), refs.pop(0), refs.pop(0)
    nw_ref, w1_ref, w2_ref = refs.pop(0), refs.pop(0), refs.pop(0)
    if has_final:
        fw_ref = refs.pop(0)
    (o_ref,) = refs
    x = x_ref[...]
    if has_mix:
        mix = _dot(ya_ref[...].astype(BF16), wo_ref[0:A_D_INNER, :]) \
            + _dot(yb_ref[...].astype(BF16), wo_ref[A_D_INNER:A_D_INNER + B_CH, :])
        x = mix + x
    xn = _rms(x, nw_ref[...]).astype(BF16)
    acc = x
    step = 1024
    for c0 in range(0, D_FF, step):
        h = jnp.maximum(_dot(xn, w1_ref[:, c0:c0 + step]), 0.0)
        acc = acc + _dot((h * h).astype(BF16), w2_ref[c0:c0 + step, :])
    if has_final:
        acc = _rms(acc, fw_ref[...])
    o_ref[...] = acc


def _layer_spec(shape, layer):
    return pl.BlockSpec((None,) + tuple(shape), lambda *_: (layer, 0, 0), pipeline_mode=pl.Buffered(1))


def _ff(x2d, mix, nw, w1, w2, layer, final_w):
    m = x2d.shape[0]
    tm = ROW_TILE if mix is not None else ROW_TILE_FINAL
    row_spec = pl.BlockSpec((tm, D_MODEL), lambda i: (i, 0))
    in_specs, args = [row_spec], [x2d]
    if mix is not None:
        ya, yb, wo = mix
        in_specs += [row_spec, row_spec, _const_spec(wo.shape)]
        args += [ya, yb, wo]
    in_specs += [_const_spec((1, D_MODEL)), _layer_spec((D_MODEL, D_FF), layer), _layer_spec((D_FF, D_MODEL), layer)]
    args += [nw, w1, w2]
    if final_w is not None:
        in_specs.append(_const_spec((1, D_MODEL)))
        args.append(final_w)
    kern = functools.partial(_ff_kernel, has_mix=mix is not None, has_final=final_w is not None)
    return pl.pallas_call(
        kern,
        out_shape=jax.ShapeDtypeStruct((m, D_MODEL), F32),
        grid=(m // tm,),
        in_specs=in_specs,
        out_specs=row_spec,
        compiler_params=_params(1),
        name="ff_mix" if mix is not None else "ff_final",
    )(*args)


def _dw_conv_slab(wbuf, i, win0, out, out0, n_out, wd_ref, bd_ref):
    bias = bd_ref[i]
    bounds = list(range(0, C_WIDTH, CONV_TAPS_PER_PASS)) + [C_WIDTH]
    n_pass = len(bounds) - 1
    for b in range(n_pass):
        k0, k1 = bounds[b], bounds[b + 1]
        taps = [jnp.broadcast_to(wd_ref[i, k:k + 1, :], (SUBLANES, LANES)) for k in range(k0, k1)]
        acc = {}
        for j in range(k0, n_out + k1 - 1):
            xj = wbuf[i, win0 + j * SUBLANES:win0 + (j + 1) * SUBLANES, :]
            for k in range(k0, k1):
                o = j - k
                if 0 <= o < n_out:
                    t = taps[k - k0] * xj
                    acc[o] = t if o not in acc else acc[o] + t
            o = j - (k1 - 1)
            if 0 <= o < n_out:
                v = acc.pop(o)
                rows = slice(out0 + o * SUBLANES, out0 + (o + 1) * SUBLANES)
                if b > 0:
                    v = out[i, rows, :] + v
                if b == n_pass - 1:
                    v = v + bias
                out[i, rows, :] = v


def _conf_kernel(*refs, n_unit, tc, has_state):
    if has_state:
        (x_ref, st_ref, nw_ref, w1_ref, b1_ref, wd_ref, bd_ref, lnw_ref, lnb_ref, w2_ref, b2_ref,
         o_ref, cst_ref, xbuf, wbuf, cvb, obuf) = refs
    else:
        (x_ref, nw_ref, w1_ref, b1_ref, wd_ref, bd_ref, lnw_ref, lnb_ref, w2_ref, b2_ref,
         o_ref, cst_ref, xbuf, wbuf, cvb, obuf) = refs
    nb = SUBLANES * n_unit
    win = (CONV_HIST + tc) * SUBLANES
    hist0 = (CONV_HIST - (C_WIDTH - 1)) * SUBLANES
    n_hist = C_WIDTH - 1
    tile = tc * SUBLANES

    for sq in range(nb):
        for i in range(N_SLAB):
            xbuf[i, pl.ds(_unit_row(sq, tc), tc, stride=SUBLANES), :] = x_ref[sq, :, i * LANES:(i + 1) * LANES]
    x = jnp.concatenate([xbuf[i] for i in range(N_SLAB)], axis=1)
    xn = _rms(x, nw_ref[...]).astype(BF16)

    if has_state:
        for u in range(n_unit):
            for i in range(N_SLAB):
                wbuf[i, u * win + hist0:u * win + hist0 + n_hist * SUBLANES, :] = \
                    st_ref[:, u * SUBLANES:(u + 1) * SUBLANES, i * LANES:(i + 1) * LANES].reshape(n_hist * SUBLANES, LANES)
    else:
        @pl.when(pl.program_id(0) == 0)
        def _():
            wbuf[:, 0:CONV_HIST * SUBLANES, :] = jnp.zeros((N_SLAB, CONV_HIST * SUBLANES, LANES), F32)

    pair = 2 * LANES
    for p in range(N_SLAB // 2):
        val = _dot(xn, w1_ref[:, p * pair:(p + 1) * pair]) + b1_ref[:, p * pair:(p + 1) * pair]
        gate = _dot(xn, w1_ref[:, C_CH + p * pair:C_CH + (p + 1) * pair]) + b1_ref[:, C_CH + p * pair:C_CH + (p + 1) * pair]
        h = val * jax.nn.sigmoid(gate)
        for i in (2 * p, 2 * p + 1):
            for u in range(n_unit):
                wbuf[i, u * win + CONV_HIST * SUBLANES:(u + 1) * win, :] = \
                    h[u * tile:(u + 1) * tile, (i - 2 * p) * LANES:(i - 2 * p + 1) * LANES]

    def conv_slab(i, carry):
        for u in range(n_unit):
            _dw_conv_slab(wbuf, i, u * win + hist0, cvb, u * tile, tc, wd_ref, bd_ref)
        return carry

    lax.fori_loop(0, N_SLAB, conv_slab, 0)

    for u in range(n_unit):
        tail = u * win + (CONV_HIST + tc - n_hist) * SUBLANES
        for i in range(N_SLAB):
            cst_ref[:, u * SUBLANES:(u + 1) * SUBLANES, i * LANES:(i + 1) * LANES] = \
                wbuf[i, tail:tail + n_hist * SUBLANES, :].reshape(n_hist, SUBLANES, LANES)
    if not has_state:
        wbuf[:, hist0:CONV_HIST * SUBLANES, :] = wbuf[:, hist0 + tile:CONV_HIST * SUBLANES + tile, :]

    c = jnp.concatenate([cvb[i] for i in range(N_SLAB)], axis=1)
    mu = jnp.mean(c, axis=-1, keepdims=True)
    cc = c - mu
    yn = cc * lax.rsqrt(jnp.mean(cc * cc, axis=-1, keepdims=True) + EPS) * lnw_ref[...] + lnb_ref[...]
    act = yn * jax.nn.sigmoid(yn)
    out = (_dot(act.astype(BF16), w2_ref[...]) + b2_ref[...]) + x
    for i in range(N_SLAB):
        obuf[i] = out[:, i * LANES:(i + 1) * LANES]
    for sq in range(nb):
        for i in range(N_SLAB):
            o_ref[sq, :, i * LANES:(i + 1) * LANES] = obuf[i, pl.ds(_unit_row(sq, tc), tc, stride=SUBLANES), :]


def _conformer(x3, state, wts, n_unit, tc):
    nb_all, seqlen, _ = x3.shape
    nb = SUBLANES * n_unit
    rows = nb * tc
    has_state = state is not None
    n_hist = C_WIDTH - 1
    if has_state:
        grid = (nb_all // nb,)
        x_map = lambda i: (i, 0, 0)
        st_map = lambda i: (0, i, 0)
    else:
        grid = (seqlen // tc,)
        x_map = lambda i: (0, i, 0)
        st_map = lambda i: (0, 0, 0)
    x_spec = pl.BlockSpec((nb, tc, C_CH), x_map)
    st_spec = pl.BlockSpec((n_hist, nb, C_CH), st_map)
    in_specs, args = [x_spec], [x3]
    if has_state:
        in_specs.append(st_spec)
        args.append(state)
    in_specs += [_const_spec((1, D_MODEL)), _const_spec((D_MODEL, 2 * C_CH)), _const_spec((1, 2 * C_CH)),
                 _const_spec((N_SLAB, C_WIDTH, LANES)), _const_spec((N_SLAB, 1, LANES)),
                 _const_spec((1, C_CH)), _const_spec((1, C_CH)),
                 _const_spec((C_CH, D_MODEL)), _const_spec((1, D_MODEL))]
    args += list(wts)
    kern = functools.partial(_conf_kernel, n_unit=n_unit, tc=tc, has_state=has_state)
    return pl.pallas_call(
        kern,
        out_shape=[jax.ShapeDtypeStruct((nb_all, seqlen, D_MODEL), F32),
                   jax.ShapeDtypeStruct((n_hist, nb_all, C_CH), F32)],
        grid=grid,
        in_specs=in_specs,
        out_specs=[x_spec, st_spec],
        scratch_shapes=[pltpu.VMEM((N_SLAB, rows, LANES), F32),
                        pltpu.VMEM((N_SLAB, n_unit * (CONV_HIST + tc) * SUBLANES, LANES), F32),
                        pltpu.VMEM((N_SLAB, rows, LANES), F32),
                        pltpu.VMEM((N_SLAB, rows, LANES), F32)],
        compiler_params=_params(1),
        name="conf_state" if has_state else "conf_prompt",
    )(*args)


def _trunk(x, states, w):
    nb, seqlen, _ = x.shape
    has_state = states is not None
    x2 = x.reshape(nb * seqlen, D_MODEL)
    if has_state:
        a_conv, a_ssm, b_re, b_im, c_conv = states
        proj = _inproj(x2, w["norm_mix0"], w["w_in"])
        ya, new_a_conv, new_a_ssm = _mamba_sample(proj, a_conv, a_ssm.reshape(nb, A_D_INNER, A_STATE),
                                                  w["a_conv"] + w["mamba"], nb, seqlen)
        yb, new_b_re, new_b_im = _s5(proj.reshape(nb, seqlen, PROJ_COLS),
                                     (b_re.reshape(nb, B_NSTATE), b_im.reshape(nb, B_NSTATE)), w["s5"],
                                     n_unit=S5_UNITS_SAMPLE, tc=seqlen)
    else:
        proj, new_a_conv = _inproj_conv(x2, w["norm_mix0"], w["w_in"], *w["a_conv"], nb, seqlen)
        proj3 = proj.reshape(nb, seqlen, PROJ_COLS)
        ya, new_a_ssm = _mamba_prompt(proj3, w["mamba"], nb, seqlen)
        yb, new_b_re, new_b_im = _s5(proj3, None, w["s5"], n_unit=1, tc=TIME_TILE)
    x2 = _ff(x2, (ya.reshape(nb * seqlen, A_D_INNER), yb.reshape(nb * seqlen, B_CH), w["w_out"]), w["norm_ff0"], w["w_ff1"], w["w_ff2"], 0, None)
    if has_state:
        x3, new_c_conv = _conformer(x2.reshape(nb, seqlen, D_MODEL), c_conv, w["conf"],
                                    n_unit=CONF_UNITS_SAMPLE, tc=seqlen)
    else:
        x3, new_c_conv = _conformer(x2.reshape(nb, seqlen, D_MODEL), None, w["conf"], n_unit=1, tc=TIME_TILE)
    y = _ff(x3.reshape(nb * seqlen, D_MODEL), None, w["norm_ff1"], w["w_ff1"], w["w_ff2"], 1, w["norm_final"])
    return (y.reshape(nb, seqlen, D_MODEL),
            new_a_conv[None],
            new_a_ssm.reshape(1, nb, A_HEADS, A_HEAD_DIM, A_STATE),
            new_b_re.reshape(1, nb, B_GROUPS, B_STATE),
            new_b_im.reshape(1, nb, B_GROUPS, B_STATE),
            jnp.swapaxes(new_c_conv, 0, 1)[None])


def _prepare_weights(norm_mix, norm_ff, norm_final, w_in_ab, a_conv_w, a_conv_b, a_dt_bias, a_log, a_d, a_norm,
                     s5_lam_re, s5_lam_im, s5_log_step, s5_b_re, s5_b_im, s5_c_re, s5_c_im, s5_d, s5_w_glu,
                     s5_b_glu, w_out_ab, c_w_pw1, c_b_pw1, c_w_dw, c_b_dw, c_ln_w, c_ln_b, c_w_pw2, c_b_pw2,
                     w_ff1, w_ff2):
    row = lambda v: v.reshape(1, -1)
    a_proj = A_D_INNER + A_CONV_DIM + A_HEADS
    w_in = w_in_ab[0]
    dt_pad = LANES - A_HEADS
    w_in_r = jnp.concatenate([w_in[:, :A_D_INNER + A_CONV_DIM], w_in[:, a_proj:],
                              w_in[:, A_D_INNER + A_CONV_DIM:a_proj],
                              jnp.zeros((D_MODEL, dt_pad), F32)], axis=1).astype(BF16)
    pad_heads = lambda v: jnp.pad(v.reshape(1, A_HEADS), ((0, 0), (0, dt_pad)))
    head_lane = jnp.arange(A_D_INNER) // A_HEAD_DIM
    expand_heads = (jnp.arange(LANES)[:, None] == head_lane[None, :]).astype(BF16)
    a_conv = (a_conv_w[0], row(a_conv_b[0]))
    mamba = (pad_heads(a_dt_bias[0]), pad_heads(a_log[0]),
             row(jnp.repeat(a_d[0], A_HEAD_DIM)), row(a_norm[0]), expand_heads, expand_heads.T)

    ab_re, ab_im, bb_re, bb_im = _s5_params(s5_lam_re[0], s5_lam_im[0], s5_log_step[0], s5_b_re[0], s5_b_im[0])
    bb = jnp.concatenate([_block_diag(bb_re, S5_BLK), _block_diag(bb_im, S5_BLK)], axis=2).astype(BF16)
    c_re_t = jnp.swapaxes(s5_c_re[0], 1, 2)
    c_im_t = jnp.swapaxes(s5_c_im[0], 1, 2)
    cc = jnp.concatenate([_block_diag(c_re_t, S5_BLK), _block_diag(-c_im_t, S5_BLK)], axis=1).astype(BF16)
    s5 = (ab_re.reshape(S5_SLABS, LANES), ab_im.reshape(S5_SLABS, LANES), bb, cc, row(s5_d[0]),
          s5_w_glu[0].astype(BF16), row(s5_b_glu[0]))

    slab_major = lambda v: jnp.swapaxes(v.reshape(-1, N_SLAB, LANES), 0, 1)
    conf = (row(norm_mix[1]), c_w_pw1[0].astype(BF16), row(c_b_pw1[0]), slab_major(c_w_dw[0]), slab_major(c_b_dw[0]),
            row(c_ln_w[0]), row(c_ln_b[0]), c_w_pw2[0].astype(BF16), row(c_b_pw2[0]))
    return dict(norm_mix0=row(norm_mix[0]), w_in=w_in_r, a_conv=a_conv, mamba=mamba, s5=s5, w_out=w_out_ab[0].astype(BF16),
                norm_ff0=row(norm_ff[0]), norm_ff1=row(norm_ff[1]), w_ff1=w_ff1.astype(BF16), w_ff2=w_ff2.astype(BF16),
                conf=conf, norm_final=row(norm_final))


def kernel(x_prompt, x_sample, state_a_conv, state_a_ssm, state_b_re, state_b_im, state_c_conv, norm_mix, norm_ff, norm_final, w_in_ab, a_conv_w, a_conv_b, a_dt_bias, a_log, a_d, a_norm, s5_lam_re, s5_lam_im, s5_log_step, s5_b_re, s5_b_im, s5_c_re, s5_c_im, s5_d, s5_w_glu, s5_b_glu, w_out_ab, c_w_pw1, c_b_pw1, c_w_dw, c_b_dw, c_ln_w, c_ln_b, c_w_pw2, c_b_pw2, w_ff1, w_ff2):
    w = _prepare_weights(norm_mix, norm_ff, norm_final, w_in_ab, a_conv_w, a_conv_b, a_dt_bias, a_log, a_d, a_norm,
                         s5_lam_re, s5_lam_im, s5_log_step, s5_b_re, s5_b_im, s5_c_re, s5_c_im, s5_d, s5_w_glu,
                         s5_b_glu, w_out_ab, c_w_pw1, c_b_pw1, c_w_dw, c_b_dw, c_ln_w, c_ln_b, c_w_pw2, c_b_pw2,
                         w_ff1, w_ff2)
    prompt = _trunk(x_prompt, None, w)
    sample = _trunk(x_sample, (state_a_conv[0], state_a_ssm[0], state_b_re[0], state_b_im[0],
                               jnp.swapaxes(state_c_conv[0], 0, 1)), w)
    return (prompt[0], sample[0]) + prompt[1:] + sample[1:]
```

```python
import functools
import math

import jax
import jax.numpy as jnp
from jax import lax
from jax.experimental import pallas as pl
from jax.experimental.pallas import tpu as pltpu

F32 = jnp.float32
BF16 = jnp.bfloat16

D_MODEL = 1024
A_HEADS = 16
A_HEAD_DIM = 64
A_D_INNER = A_HEADS * A_HEAD_DIM
A_GROUPS = 4
A_STATE = 128
A_CONV = 4
A_CONV_DIM = A_D_INNER + 2 * A_GROUPS * A_STATE
A_CHUNK = 128
B_CH = 1024
B_GROUP_CH = 16
B_GROUPS = B_CH // B_GROUP_CH
B_STATE = 64
B_NSTATE = B_GROUPS * B_STATE
C_CH = D_MODEL
C_WIDTH = 31
D_FF = 4 * D_MODEL
EPS = 1e-6

SUBLANES = 8
LANES = 128
VMEM_LIMIT_BYTES = 56 * 1024 * 1024

COL_Z = 0
COL_X = A_D_INNER
COL_BC = 2 * A_D_INNER
COL_U = 3 * A_D_INNER
COL_DT = 4 * A_D_INNER
PROJ_COLS = COL_DT + LANES

NEG_BIG = -1e30
N_SLAB = B_CH // LANES
S5_SLABS = B_NSTATE // LANES
S5_BLK = 8
ROW_TILE = 512
ROW_TILE_FINAL = 1024
SSD_SEQS_PROMPT = 8
SSD_SEQS_SAMPLE = 8
S5_TIME_TILE = 64
CONF_TIME_TILE = 128
S5_UNITS_SAMPLE = 8
CONF_UNITS_SAMPLE = 4
CONV_HIST = 32
CONV_TAPS_PER_PASS = 16


def _params(n_grid):
    return pltpu.CompilerParams(dimension_semantics=("arbitrary",) * n_grid,
                                vmem_limit_bytes=VMEM_LIMIT_BYTES)


def _const_spec(shape):
    nd = len(shape)
    return pl.BlockSpec(shape, lambda *_: (0,) * nd, pipeline_mode=pl.Buffered(1))


def _rms(x, w):
    return x * lax.rsqrt(jnp.mean(x * x, axis=-1, keepdims=True) + EPS) * w


def _dot(a, b):
    return jnp.dot(a, b, preferred_element_type=F32)


def _dot_nt(a, b):
    return lax.dot_general(a, b, (((1,), (1,)), ((), ())), preferred_element_type=F32)


def _dot_tn(a, b):
    return lax.dot_general(a, b, (((0,), (0,)), ((), ())), preferred_element_type=F32)


def _split_bf16(v):
    hi = v.astype(BF16)
    lo = (v - hi.astype(F32)).astype(BF16)
    return hi, lo


def _inproj_kernel(x_ref, nw_ref, w_ref, o_ref):
    xn = _rms(x_ref[...], nw_ref[...]).astype(BF16)
    n = o_ref.shape[1]
    step = 512
    for c0 in range(0, n, step):
        c1 = min(c0 + step, n)
        o_ref[:, c0:c1] = _dot(xn, w_ref[:, c0:c1])


def _inproj(x2d, nw, w):
    m = x2d.shape[0]
    tm = ROW_TILE
    return pl.pallas_call(
        _inproj_kernel,
        out_shape=jax.ShapeDtypeStruct((m, PROJ_COLS), F32),
        grid=(m // tm,),
        in_specs=[pl.BlockSpec((tm, D_MODEL), lambda i: (i, 0)),
                  _const_spec((1, D_MODEL)),
                  _const_spec((D_MODEL, PROJ_COLS))],
        out_specs=pl.BlockSpec((tm, PROJ_COLS), lambda i: (i, 0)),
        compiler_params=_params(1),
        name="inproj",
    )(x2d, nw, w)


def _inproj_conv_kernel(x_ref, nw_ref, w_ref, cw_ref, cb_ref, o_ref, cst_ref, win_ref, *, tiles_per_seq):
    tm = x_ref.shape[0]
    lo = SUBLANES - (A_CONV - 1)

    @pl.when(pl.program_id(0) % tiles_per_seq == 0)
    def _():
        win_ref[...] = jnp.zeros(win_ref.shape, F32)

    xn = _rms(x_ref[...], nw_ref[...]).astype(BF16)
    step = 512
    plain = list(range(COL_Z, COL_X, step)) + list(range(COL_U, PROJ_COLS, step))

    def plain_chunk():
        c0 = plain.pop(0)
        c1 = min(c0 + step, PROJ_COLS)
        o_ref[:, c0:c1] = _dot(xn, w_ref[:, c0:c1])

    for c0 in range(COL_X, COL_U, step):
        w0 = c0 - COL_X
        raw = _dot(xn, w_ref[:, c0:c0 + step])
        plain_chunk()
        full = jnp.concatenate([win_ref[:, w0:w0 + step], raw], axis=0)
        conv = raw * cw_ref[A_CONV - 1:A_CONV, w0:w0 + step]
        for k in range(A_CONV - 1):
            conv = conv + full[lo + k:lo + k + tm] * cw_ref[k:k + 1, w0:w0 + step]
        xc = conv + cb_ref[:, w0:w0 + step]
        o_ref[:, c0:c0 + step] = xc * jax.nn.sigmoid(xc)
        win_ref[:, w0:w0 + step] = raw[tm - SUBLANES:tm]
    cst_ref[0] = win_ref[lo:SUBLANES, :]
    while plain:
        plain_chunk()


def _inproj_conv(x2d, nw, w, cw, cb, nb, seqlen):
    tm = ROW_TILE
    tiles_per_seq = seqlen // tm
    kern = functools.partial(_inproj_conv_kernel, tiles_per_seq=tiles_per_seq)
    return pl.pallas_call(
        kern,
        out_shape=[jax.ShapeDtypeStruct((nb * seqlen, PROJ_COLS), F32),
                   jax.ShapeDtypeStruct((nb, A_CONV - 1, A_CONV_DIM), F32)],
        grid=(nb * tiles_per_seq,),
        in_specs=[pl.BlockSpec((tm, D_MODEL), lambda i: (i, 0)),
                  _const_spec((1, D_MODEL)),
                  _const_spec((D_MODEL, PROJ_COLS)),
                  _const_spec((A_CONV, A_CONV_DIM)),
                  _const_spec((1, A_CONV_DIM))],
        out_specs=[pl.BlockSpec((tm, PROJ_COLS), lambda i: (i, 0)),
                   pl.BlockSpec((1, A_CONV - 1, A_CONV_DIM), lambda i: (i // tiles_per_seq, 0, 0))],
        scratch_shapes=[pltpu.VMEM((SUBLANES, A_CONV_DIM), F32)],
        compiler_params=_params(1),
        name="inproj_conv",
    )(x2d, nw, w, cw, cb)


def _conv_silu_window(xs_ref, bc_ref, cw_ref, cb_ref, win_ref, *, n_seq, q):
    raw = jnp.concatenate([xs_ref[...], bc_ref[...]], axis=1)
    for s in range(n_seq):
        win_ref[s, SUBLANES:SUBLANES + q, :] = raw[s * q:(s + 1) * q]
    conv = jnp.zeros((n_seq * q, A_CONV_DIM), F32)
    for k in range(A_CONV):
        off = SUBLANES - (A_CONV - 1) + k
        tap = jnp.concatenate([win_ref[s, off:off + q, :] for s in range(n_seq)], axis=0)
        conv = conv + tap * cw_ref[k:k + 1, :]
    xc = conv + cb_ref[...]
    return xc * jax.nn.sigmoid(xc)


def _ssd_tile(xc, z_ref, dt_ref, dtb_ref, alog_ref, dsk_ref, nw_ref, eh_ref, eht_ref,
              h_in_ref, h_out_ref, ya_ref, *, n_seq, q):
    tm = n_seq * q
    x = xc[:, :A_D_INNER]
    bm = xc[:, A_D_INNER:A_D_INNER + A_GROUPS * A_STATE].astype(BF16)
    cm = xc[:, A_D_INNER + A_GROUPS * A_STATE:].astype(BF16)
    xb = x.astype(BF16)

    dt = jax.nn.softplus(dt_ref[...] + dtb_ref[...])
    a = -jnp.exp(alog_ref[...])
    da = dt * a
    ii = lax.broadcasted_iota(jnp.int32, (tm, tm), 0)
    jj = lax.broadcasted_iota(jnp.int32, (tm, tm), 1)
    causal = jj <= ii
    if n_seq > 1:
        causal = causal & ((ii // q) == (jj // q))
    tri = jnp.where(causal, 1.0, 0.0).astype(F32)
    cum = jnp.dot(tri, da, precision=lax.Precision.HIGHEST, preferred_element_type=F32)
    cum_t = cum.T
    dt_t = dt.T
    last = jnp.concatenate(
        [jnp.broadcast_to(cum[s * q + q - 1:s * q + q, :], (q, LANES)) for s in range(n_seq)], axis=0)

    eh = eh_ref[...]

    def expand(v):
        hi, lo = _split_bf16(v)
        return _dot(hi, eh) + _dot(lo, eh)

    exp_cum = expand(jnp.exp(cum))
    w_state = expand(jnp.exp(last - cum) * dt)
    xw = (x * w_state).astype(BF16)
    lane = lax.broadcasted_iota(jnp.int32, (tm, LANES), 1)
    low_half = lane < A_HEAD_DIM

    decs = []
    for s in range(n_seq):
        col = s * q + q - 1
        lb = jnp.broadcast_to(jnp.exp(cum_t[:, col:col + 1]), (LANES, LANES))
        hi, lo = _split_bf16(lb)
        decs.append(_dot(eht_ref[...], hi) + _dot(eht_ref[...], lo))

    heads_per_group = A_HEADS // A_GROUPS
    gw = heads_per_group * A_HEAD_DIM
    y_parts = []
    for g in range(A_GROUPS):
        cg = cm[:, g * A_STATE:(g + 1) * A_STATE]
        bg = bm[:, g * A_STATE:(g + 1) * A_STATE]
        scores = _dot_nt(cg, bg)
        pair_out = []
        for pr in range(heads_per_group // 2):
            lane0 = g * gw + pr * LANES
            xp = xb[:, lane0:lane0 + LANES]
            acc = jnp.zeros((tm, LANES), F32)
            for half in range(2):
                e = g * heads_per_group + pr * 2 + half
                seg = cum[:, e:e + 1] - cum_t[e:e + 1, :]
                dec = jnp.exp(jnp.where(causal, seg, NEG_BIG))
                m = (scores * dec * dt_t[e:e + 1, :]).astype(BF16)
                xh = jnp.where(low_half if half == 0 else jnp.logical_not(low_half), xp, jnp.zeros_like(xp))
                acc = acc + _dot(m, xh)
            pair_out.append(acc)
        y_diag = jnp.concatenate(pair_out, axis=1)
        offs = []
        for s in range(n_seq):
            r0, r1 = s * q, (s + 1) * q
            h_old = h_in_ref[s, g * gw:(g + 1) * gw, :]
            offs.append(_dot_nt(cg[r0:r1], h_old.astype(BF16)))
            st = _dot_tn(xw[r0:r1, g * gw:(g + 1) * gw], bg[r0:r1])
            h_out_ref[s, g * gw:(g + 1) * gw, :] = h_old * decs[s][g * gw:(g + 1) * gw, :] + st
        y_off = jnp.concatenate(offs, axis=0) * exp_cum[:, g * gw:(g + 1) * gw]
        y_parts.append(y_diag + y_off)
    y = jnp.concatenate(y_parts, axis=1) + dsk_ref[...] * x

    z = z_ref[...]
    gt = y * (z * jax.nn.sigmoid(z))
    outs = []
    for g in range(A_GROUPS):
        gg = gt[:, g * gw:(g + 1) * gw]
        outs.append(gg * lax.rsqrt(jnp.mean(gg * gg, axis=-1, keepdims=True) + EPS))
    ya_ref[...] = (jnp.concatenate(outs, axis=1) * nw_ref[...]).astype(ya_ref.dtype)


def _mamba_prompt_kernel(z_ref, xs_ref, bc_ref, dt_ref, dtb_ref, alog_ref, dsk_ref, nw_ref, eh_ref, eht_ref,
                         ya_ref, h_ref):
    @pl.when(pl.program_id(1) == 0)
    def _():
        h_ref[...] = jnp.zeros(h_ref.shape, F32)

    for g in range(z_ref.shape[0]):
        xc = jnp.concatenate([xs_ref[g], bc_ref[g]], axis=1)
        h_g = h_ref.at[pl.ds(g, 1)]
        _ssd_tile(xc, z_ref.at[g], dt_ref.at[g], dtb_ref, alog_ref, dsk_ref, nw_ref, eh_ref, eht_ref,
                  h_g, h_g, ya_ref.at[g], n_seq=1, q=A_CHUNK)


def _mamba_sample_kernel(z_ref, xs_ref, bc_ref, dt_ref, cin_ref, hin_ref, cw_ref, cb_ref, dtb_ref, alog_ref,
                         dsk_ref, nw_ref, eh_ref, eht_ref, ya_ref, cout_ref, h_ref, win_ref, *, n_seq, q):
    lo = SUBLANES - (A_CONV - 1)
    for s in range(n_seq):
        win_ref[s, lo:SUBLANES, :] = cin_ref[s]
    xc = _conv_silu_window(xs_ref, bc_ref, cw_ref, cb_ref, win_ref, n_seq=n_seq, q=q)
    _ssd_tile(xc, z_ref, dt_ref, dtb_ref, alog_ref, dsk_ref, nw_ref, eh_ref, eht_ref,
              hin_ref, h_ref, ya_ref, n_seq=n_seq, q=q)
    for s in range(n_seq):
        cout_ref[s] = win_ref[s, SUBLANES + q - (A_CONV - 1):SUBLANES + q, :]


def _conv_weights_specs():
    return [_const_spec((A_CONV, A_CONV_DIM)), _const_spec((1, A_CONV_DIM))]


def _mamba_weights_specs():
    return [_const_spec((1, LANES)), _const_spec((1, LANES)), _const_spec((1, A_D_INNER)),
            _const_spec((1, A_D_INNER)), _const_spec((LANES, A_D_INNER)), _const_spec((A_D_INNER, LANES))]


def _mamba_prompt(proj3, wts, nb, seqlen):
    q = A_CHUNK
    nc = seqlen // q
    n_par = SSD_SEQS_PROMPT
    blk = lambda col_block: pl.BlockSpec((n_par, q, A_D_INNER), lambda b, c: (b, c, col_block))
    return pl.pallas_call(
        _mamba_prompt_kernel,
        out_shape=[jax.ShapeDtypeStruct((nb, seqlen, A_D_INNER), BF16),
                   jax.ShapeDtypeStruct((nb, A_D_INNER, A_STATE), F32)],
        grid=(nb // n_par, nc),
        in_specs=[blk(COL_Z // A_D_INNER), blk(COL_X // A_D_INNER), blk(COL_BC // A_D_INNER),
                  pl.BlockSpec((n_par, q, LANES), lambda b, c: (b, c, COL_DT // LANES))] + _mamba_weights_specs(),
        out_specs=[pl.BlockSpec((n_par, q, A_D_INNER), lambda b, c: (b, c, 0)),
                   pl.BlockSpec((n_par, A_D_INNER, A_STATE), lambda b, c: (b, 0, 0))],
        compiler_params=_params(2),
        name="mamba_prompt",
    )(proj3, proj3, proj3, proj3, *wts)


def _mamba_sample(proj, conv_state, ssm_state, wts, nb, seqlen):
    n_seq = SSD_SEQS_SAMPLE
    q = seqlen
    tm = n_seq * q
    wide = A_D_INNER // LANES
    kern = functools.partial(_mamba_sample_kernel, n_seq=n_seq, q=q)
    return pl.pallas_call(
        kern,
        out_shape=[jax.ShapeDtypeStruct((nb * seqlen, A_D_INNER), BF16),
                   jax.ShapeDtypeStruct((nb, A_CONV - 1, A_CONV_DIM), F32),
                   jax.ShapeDtypeStruct((nb, A_D_INNER, A_STATE), F32)],
        grid=(nb // n_seq,),
        in_specs=[pl.BlockSpec((tm, A_D_INNER), lambda i: (i, 0)),
                  pl.BlockSpec((tm, A_D_INNER), lambda i: (i, 1)),
                  pl.BlockSpec((tm, A_D_INNER), lambda i: (i, 2)),
                  pl.BlockSpec((tm, LANES), lambda i: (i, 4 * wide)),
                  pl.BlockSpec((n_seq, A_CONV - 1, A_CONV_DIM), lambda i: (i, 0, 0)),
                  pl.BlockSpec((n_seq, A_D_INNER, A_STATE), lambda i: (i, 0, 0))]
                 + _conv_weights_specs() + _mamba_weights_specs(),
        out_specs=[pl.BlockSpec((tm, A_D_INNER), lambda i: (i, 0)),
                   pl.BlockSpec((n_seq, A_CONV - 1, A_CONV_DIM), lambda i: (i, 0, 0)),
                   pl.BlockSpec((n_seq, A_D_INNER, A_STATE), lambda i: (i, 0, 0))],
        scratch_shapes=[pltpu.VMEM((n_seq, SUBLANES + q, A_CONV_DIM), F32)],
        compiler_params=_params(1),
        name="mamba_sample",
    )(proj, proj, proj, proj, conv_state, ssm_state, *wts)


def _s5_param_kernel(lr_ref, li_ref, ls_ref, br_ref, bi_ref, are_ref, aim_ref, bbr_ref, bbi_ref):
    lr = lr_ref[...]
    li = li_ref[...]
    step = jnp.exp(ls_ref[...])
    mag = jnp.exp(lr * step)
    ab_re = mag * jnp.cos(li * step)
    ab_im = mag * jnp.sin(li * step)
    den = lr * lr + li * li
    k_re = ((ab_re - 1.0) * lr + ab_im * li) / den
    k_im = (ab_im * lr - (ab_re - 1.0) * li) / den
    br = br_ref[...]
    bi = bi_ref[...]
    are_ref[...] = ab_re
    aim_ref[...] = ab_im
    bbr_ref[...] = k_re * br - k_im * bi
    bbi_ref[...] = k_re * bi + k_im * br


def _s5_params(lam_re, lam_im, log_step, b_re, b_im):
    g, p, c = B_GROUPS, B_STATE, B_GROUP_CH
    return pl.pallas_call(
        _s5_param_kernel,
        out_shape=[jax.ShapeDtypeStruct((g, 1, p), F32), jax.ShapeDtypeStruct((g, 1, p), F32),
                   jax.ShapeDtypeStruct((g, c, p), F32), jax.ShapeDtypeStruct((g, c, p), F32)],
        name="s5_params",
    )(lam_re.reshape(g, 1, p), lam_im.reshape(g, 1, p), log_step.reshape(g, 1, 1),
      jnp.swapaxes(b_re, 1, 2), jnp.swapaxes(b_im, 1, 2))


def _block_diag(m, blk):
    g, r, c = m.shape
    eye = jnp.eye(blk, dtype=m.dtype)
    t = m.reshape(g // blk, blk, r, c)
    return (t[:, :, :, None, :] * eye[None, :, None, :, None]).reshape(g // blk, blk * r, blk * c)


def _gelu_tanh(x):
    return x * (0.5 * (1.0 + jnp.tanh(math.sqrt(2.0 / math.pi) * (x + 0.044715 * (x * x * x)))))


def _unit_row(sq, tc):
    u, b = divmod(sq, SUBLANES)
    return u * tc * SUBLANES + b


def _s5_kernel(*refs, n_unit, tc, has_state):
    if has_state:
        (u_ref, sre_ref, sim_ref, are_ref, aim_ref, bb_ref, cc_ref, dsk_ref, wg_ref, bg_ref,
         o_ref, hre_ref, him_ref, ubuf, buf, hst, obuf) = refs
    else:
        (u_ref, are_ref, aim_ref, bb_ref, cc_ref, dsk_ref, wg_ref, bg_ref,
         o_ref, hre_ref, him_ref, ubuf, buf, hst, obuf) = refs
    nb = SUBLANES * n_unit
    half = S5_SLABS

    for sq in range(nb):
        for i in range(N_SLAB):
            ubuf[i, pl.ds(_unit_row(sq, tc), tc, stride=SUBLANES), :] = u_ref[sq, :, i * LANES:(i + 1) * LANES]

    if has_state:
        for j in range(S5_SLABS):
            hst[j] = sre_ref[:, j * LANES:(j + 1) * LANES]
            hst[half + j] = sim_ref[:, j * LANES:(j + 1) * LANES]
    else:
        @pl.when(pl.program_id(0) == 0)
        def _():
            hst[...] = jnp.zeros(hst.shape, F32)

    per = S5_SLABS // N_SLAB

    def b_proj(i):
        r = _dot(ubuf[i].astype(BF16), bb_ref[i])
        for k in range(per):
            buf[per * i + k] = r[:, k * LANES:(k + 1) * LANES]
            buf[half + per * i + k] = r[:, (per + k) * LANES:(per + k + 1) * LANES]

    def scan(i):
        a_re = [jnp.broadcast_to(are_ref[per * i + k:per * i + k + 1, :], (SUBLANES, LANES)) for k in range(per)]
        a_im = [jnp.broadcast_to(aim_ref[per * i + k:per * i + k + 1, :], (SUBLANES, LANES)) for k in range(per)]
        for u in range(n_unit):
            rows = slice(u * SUBLANES, (u + 1) * SUBLANES)
            h_re = [hst[per * i + k, rows, :] for k in range(per)]
            h_im = [hst[half + per * i + k, rows, :] for k in range(per)]
            for t in range(tc):
                r0 = (u * tc + t) * SUBLANES
                for k in range(per):
                    j = per * i + k
                    n_re = a_re[k] * h_re[k] - a_im[k] * h_im[k] + buf[j, r0:r0 + SUBLANES, :]
                    n_im = a_re[k] * h_im[k] + a_im[k] * h_re[k] + buf[half + j, r0:r0 + SUBLANES, :]
                    buf[j, r0:r0 + SUBLANES, :] = n_re
                    buf[half + j, r0:r0 + SUBLANES, :] = n_im
                    h_re[k], h_im[k] = n_re, n_im
            for k in range(per):
                hst[per * i + k, rows, :] = h_re[k]
                hst[half + per * i + k, rows, :] = h_im[k]

    def c_proj(i):
        lhs = jnp.concatenate([buf[per * i + k] for k in range(per)]
                              + [buf[half + per * i + k] for k in range(per)], axis=1).astype(BF16)
        yi = _dot(lhs, cc_ref[i]) + dsk_ref[:, i * LANES:(i + 1) * LANES] * ubuf[i]
        return _gelu_tanh(yi)

    ys = []
    b_proj(0)
    for i in range(N_SLAB):
        if i + 1 < N_SLAB:
            b_proj(i + 1)
        scan(i)
        ys.append(c_proj(i))

    for j in range(S5_SLABS):
        hre_ref[:, j * LANES:(j + 1) * LANES] = hst[j]
        him_ref[:, j * LANES:(j + 1) * LANES] = hst[half + j]

    y_bf = jnp.concatenate(ys, axis=1).astype(BF16)
    pair = 2 * LANES
    for c in range(N_SLAB // 2):
        gate = jax.nn.sigmoid(_dot(y_bf, wg_ref[:, c * pair:(c + 1) * pair]) + bg_ref[:, c * pair:(c + 1) * pair])
        for k in range(2):
            obuf[2 * c + k] = ys[2 * c + k] * gate[:, k * LANES:(k + 1) * LANES]
    for sq in range(nb):
        for i in range(N_SLAB):
            o_ref[sq, :, i * LANES:(i + 1) * LANES] = \
                obuf[i, pl.ds(_unit_row(sq, tc), tc, stride=SUBLANES), :].astype(o_ref.dtype)


def _s5(proj3, state, wts, n_unit, tc):
    nb_all, seqlen, _ = proj3.shape
    nb = SUBLANES * n_unit
    rows = nb * tc
    has_state = state is not None
    if has_state:
        grid = (nb_all // nb,)
        u_map = lambda i: (i, 0, COL_U // B_CH)
        st_map = lambda i: (i, 0)
    else:
        grid = (seqlen // tc,)
        u_map = lambda i: (0, i, COL_U // B_CH)
        st_map = lambda i: (0, 0)
    st_spec = pl.BlockSpec((nb, B_NSTATE), st_map)
    in_specs = [pl.BlockSpec((nb, tc, B_CH), u_map)]
    args = [proj3]
    if has_state:
        in_specs += [st_spec, st_spec]
        args += list(state)
    in_specs += [_const_spec((S5_SLABS, LANES)), _const_spec((S5_SLABS, LANES)),
                 _const_spec((N_SLAB, LANES, 2 * S5_BLK * B_STATE)),
                 _const_spec((N_SLAB, 2 * S5_BLK * B_STATE, LANES)),
                 _const_spec((1, B_CH)), _const_spec((B_CH, B_CH)), _const_spec((1, B_CH))]
    args += list(wts)
    kern = functools.partial(_s5_kernel, n_unit=n_unit, tc=tc, has_state=has_state)
    return pl.pallas_call(
        kern,
        out_shape=[jax.ShapeDtypeStruct((nb_all, seqlen, B_CH), BF16 if tc % (2 * SUBLANES) == 0 else F32),
                   jax.ShapeDtypeStruct((nb_all, B_NSTATE), F32),
                   jax.ShapeDtypeStruct((nb_all, B_NSTATE), F32)],
        grid=grid,
        in_specs=in_specs,
        out_specs=[pl.BlockSpec((nb, tc, B_CH), (lambda i: (i, 0, 0)) if has_state else (lambda i: (0, i, 0))),
                   st_spec, st_spec],
        scratch_shapes=[pltpu.VMEM((N_SLAB, rows, LANES), F32),
                        pltpu.VMEM((2 * S5_SLABS, rows, LANES), F32),
                        pltpu.VMEM((2 * S5_SLABS, nb, LANES), F32),
                        pltpu.VMEM((N_SLAB, rows, LANES), F32)],
        compiler_params=_params(1),
        name="s5_state" if has_state else "s5_prompt",
    )(*args)


def _ff_kernel(*refs, has_mix, has_final):
    refs = list(refs)
    x_ref = refs.pop(0)
    if has_mix:
        ya_ref, yb_ref, wo_ref = refs.pop(0), refs.pop(0), refs.pop(0)
    nw_ref, w1_ref, w2_ref = refs.pop(0), refs.pop(0), refs.pop(0)
    if has_final:
        fw_ref = refs.pop(0)
    (o_ref,) = refs
    x = x_ref[...]
    if has_mix:
        mix = _dot(ya_ref[...].astype(BF16), wo_ref[0:A_D_INNER, :]) \
            + _dot(yb_ref[...].astype(BF16), wo_ref[A_D_INNER:A_D_INNER + B_CH, :])
        x = mix + x
    xn = _rms(x, nw_ref[...]).astype(BF16)
    acc = x
    step = 1024
    for c0 in range(0, D_FF, step):
        h = jnp.maximum(_dot(xn, w1_ref[:, c0:c0 + step]), 0.0)
        acc = acc + _dot((h * h).astype(BF16), w2_ref[c0:c0 + step, :])
    if has_final:
        acc = _rms(acc, fw_ref[...])
    o_ref[...] = acc


def _layer_spec(shape, layer):
    return pl.BlockSpec((None,) + tuple(shape), lambda *_: (layer, 0, 0), pipeline_mode=pl.Buffered(1))


def _ff(x2d, mix, nw, w1, w2, layer, final_w):
    m = x2d.shape[0]
    tm = ROW_TILE if mix is not None else ROW_TILE_FINAL
    row_spec = pl.BlockSpec((tm, D_MODEL), lambda i: (i, 0))
    in_specs, args = [row_spec], [x2d]
    if mix is not None:
        ya, yb, wo = mix
        in_specs += [row_spec, row_spec, _const_spec(wo.shape)]
        args += [ya, yb, wo]
    in_specs += [_const_spec((1, D_MODEL)), _layer_spec((D_MODEL, D_FF), layer), _layer_spec((D_FF, D_MODEL), layer)]
    args += [nw, w1, w2]
    if final_w is not None:
        in_specs.append(_const_spec((1, D_MODEL)))
        args.append(final_w)
    kern = functools.partial(_ff_kernel, has_mix=mix is not None, has_final=final_w is not None)
    return pl.pallas_call(
        kern,
        out_shape=jax.ShapeDtypeStruct((m, D_MODEL), F32),
        grid=(m // tm,),
        in_specs=in_specs,
        out_specs=row_spec,
        compiler_params=_params(1),
        name="ff_mix" if mix is not None else "ff_final",
    )(*args)


def _dw_conv_slab(wbuf, i, win0, out, out0, n_out, wd_ref, bd_ref):
    bias = bd_ref[i]
    bounds = list(range(0, C_WIDTH, CONV_TAPS_PER_PASS)) + [C_WIDTH]
    n_pass = len(bounds) - 1
    for b in range(n_pass):
        k0, k1 = bounds[b], bounds[b + 1]
        taps = [jnp.broadcast_to(wd_ref[i, k:k + 1, :], (SUBLANES, LANES)) for k in range(k0, k1)]
        acc = {}
        for j in range(k0, n_out + k1 - 1):
            xj = wbuf[i, win0 + j * SUBLANES:win0 + (j + 1) * SUBLANES, :]
            for k in range(k0, k1):
                o = j - k
                if 0 <= o < n_out:
                    t = taps[k - k0] * xj
                    acc[o] = t if o not in acc else acc[o] + t
            o = j - (k1 - 1)
            if 0 <= o < n_out:
                v = acc.pop(o)
                rows = slice(out0 + o * SUBLANES, out0 + (o + 1) * SUBLANES)
                if b > 0:
                    v = out[i, rows, :] + v
                if b == n_pass - 1:
                    v = v + bias
                out[i, rows, :] = v


def _conf_kernel(*refs, n_unit, tc, has_state):
    if has_state:
        (x_ref, st_ref, nw_ref, w1_ref, b1_ref, wd_ref, bd_ref, lnw_ref, lnb_ref, w2_ref, b2_ref,
         o_ref, cst_ref, xbuf, wbuf, cvb, obuf) = refs
    else:
        (x_ref, nw_ref, w1_ref, b1_ref, wd_ref, bd_ref, lnw_ref, lnb_ref, w2_ref, b2_ref,
         o_ref, cst_ref, xbuf, wbuf, cvb, obuf) = refs
    nb = SUBLANES * n_unit
    win = (CONV_HIST + tc) * SUBLANES
    hist0 = (CONV_HIST - (C_WIDTH - 1)) * SUBLANES
    n_hist = C_WIDTH - 1
    tile = tc * SUBLANES

    for sq in range(nb):
        for i in range(N_SLAB):
            xbuf[i, pl.ds(_unit_row(sq, tc), tc, stride=SUBLANES), :] = x_ref[sq, :, i * LANES:(i + 1) * LANES]
    x = jnp.concatenate([xbuf[i] for i in range(N_SLAB)], axis=1)
    xn = _rms(x, nw_ref[...]).astype(BF16)

    if has_state:
        for u in range(n_unit):
            for i in range(N_SLAB):
                wbuf[i, u * win + hist0:u * win + hist0 + n_hist * SUBLANES, :] = \
                    st_ref[:, u * SUBLANES:(u + 1) * SUBLANES, i * LANES:(i + 1) * LANES].reshape(n_hist * SUBLANES, LANES)
    else:
        @pl.when(pl.program_id(0) == 0)
        def _():
            wbuf[:, 0:CONV_HIST * SUBLANES, :] = jnp.zeros((N_SLAB, CONV_HIST * SUBLANES, LANES), F32)

    pair = 2 * LANES
    for p in range(N_SLAB // 2):
        val = _dot(xn, w1_ref[:, p * pair:(p + 1) * pair]) + b1_ref[:, p * pair:(p + 1) * pair]
        gate = _dot(xn, w1_ref[:, C_CH + p * pair:C_CH + (p + 1) * pair]) + b1_ref[:, C_CH + p * pair:C_CH + (p + 1) * pair]
        h = val * jax.nn.sigmoid(gate)
        for i in (2 * p, 2 * p + 1):
            for u in range(n_unit):
                wbuf[i, u * win + CONV_HIST * SUBLANES:(u + 1) * win, :] = \
                    h[u * tile:(u + 1) * tile, (i - 2 * p) * LANES:(i - 2 * p + 1) * LANES]

    def conv_slab(i, carry):
        for u in range(n_unit):
            _dw_conv_slab(wbuf, i, u * win + hist0, cvb, u * tile, tc, wd_ref, bd_ref)
        return carry

    lax.fori_loop(0, N_SLAB, conv_slab, 0)

    for u in range(n_unit):
        tail = u * win + (CONV_HIST + tc - n_hist) * SUBLANES
        for i in range(N_SLAB):
            cst_ref[:, u * SUBLANES:(u + 1) * SUBLANES, i * LANES:(i + 1) * LANES] = \
                wbuf[i, tail:tail + n_hist * SUBLANES, :].reshape(n_hist, SUBLANES, LANES)
    if not has_state:
        wbuf[:, hist0:CONV_HIST * SUBLANES, :] = wbuf[:, hist0 + tile:CONV_HIST * SUBLANES + tile, :]

    c = jnp.concatenate([cvb[i] for i in range(N_SLAB)], axis=1)
    mu = jnp.mean(c, axis=-1, keepdims=True)
    cc = c - mu
    yn = cc * lax.rsqrt(jnp.mean(cc * cc, axis=-1, keepdims=True) + EPS) * lnw_ref[...] + lnb_ref[...]
    act = yn * jax.nn.sigmoid(yn)
    out = (_dot(act.astype(BF16), w2_ref[...]) + b2_ref[...]) + x
    for i in range(N_SLAB):
        obuf[i] = out[:, i * LANES:(i + 1) * LANES]
    for sq in range(nb):
        for i in range(N_SLAB):
            o_ref[sq, :, i * LANES:(i + 1) * LANES] = obuf[i, pl.ds(_unit_row(sq, tc), tc, stride=SUBLANES), :]


def _conformer(x3, state, wts, n_unit, tc):
    nb_all, seqlen, _ = x3.shape
    nb = SUBLANES * n_unit
    rows = nb * tc
    has_state = state is not None
    n_hist = C_WIDTH - 1
    if has_state:
        grid = (nb_all // nb,)
        x_map = lambda i: (i, 0, 0)
        st_map = lambda i: (0, i, 0)
    else:
        grid = (seqlen // tc,)
        x_map = lambda i: (0, i, 0)
        st_map = lambda i: (0, 0, 0)
    x_spec = pl.BlockSpec((nb, tc, C_CH), x_map)
    st_spec = pl.BlockSpec((n_hist, nb, C_CH), st_map)
    in_specs, args = [x_spec], [x3]
    if has_state:
        in_specs.append(st_spec)
        args.append(state)
    in_specs += [_const_spec((1, D_MODEL)), _const_spec((D_MODEL, 2 * C_CH)), _const_spec((1, 2 * C_CH)),
                 _const_spec((N_SLAB, C_WIDTH, LANES)), _const_spec((N_SLAB, 1, LANES)),
                 _const_spec((1, C_CH)), _const_spec((1, C_CH)),
                 _const_spec((C_CH, D_MODEL)), _const_spec((1, D_MODEL))]
    args += list(wts)
    kern = functools.partial(_conf_kernel, n_unit=n_unit, tc=tc, has_state=has_state)
    return pl.pallas_call(
        kern,
        out_shape=[jax.ShapeDtypeStruct((nb_all, seqlen, D_MODEL), F32),
                   jax.ShapeDtypeStruct((n_hist, nb_all, C_CH), F32)],
        grid=grid,
        in_specs=in_specs,
        out_specs=[x_spec, st_spec],
        scratch_shapes=[pltpu.VMEM((N_SLAB, rows, LANES), F32),
                        pltpu.VMEM((N_SLAB, n_unit * (CONV_HIST + tc) * SUBLANES, LANES), F32),
                        pltpu.VMEM((N_SLAB, rows, LANES), F32),
                        pltpu.VMEM((N_SLAB, rows, LANES), F32)],
        compiler_params=_params(1),
        name="conf_state" if has_state else "conf_prompt",
    )(*args)


def _trunk(x, states, w):
    nb, seqlen, _ = x.shape
    has_state = states is not None
    x2 = x.reshape(nb * seqlen, D_MODEL)
    if has_state:
        a_conv, a_ssm, b_re, b_im, c_conv = states
        proj = _inproj(x2, w["norm_mix0"], w["w_in"])
        ya, new_a_conv, new_a_ssm = _mamba_sample(proj, a_conv, a_ssm.reshape(nb, A_D_INNER, A_STATE),
                                                  w["a_conv"] + w["mamba"], nb, seqlen)
        yb, new_b_re, new_b_im = _s5(proj.reshape(nb, seqlen, PROJ_COLS),
                                     (b_re.reshape(nb, B_NSTATE), b_im.reshape(nb, B_NSTATE)), w["s5"],
                                     n_unit=S5_UNITS_SAMPLE, tc=seqlen)
    else:
        proj, new_a_conv = _inproj_conv(x2, w["norm_mix0"], w["w_in"], *w["a_conv"], nb, seqlen)
        proj3 = proj.reshape(nb, seqlen, PROJ_COLS)
        ya, new_a_ssm = _mamba_prompt(proj3, w["mamba"], nb, seqlen)
        yb, new_b_re, new_b_im = _s5(proj3, None, w["s5"], n_unit=1, tc=S5_TIME_TILE)
    x2 = _ff(x2, (ya.reshape(nb * seqlen, A_D_INNER), yb.reshape(nb * seqlen, B_CH), w["w_out"]), w["norm_ff0"], w["w_ff1"], w["w_ff2"], 0, None)
    if has_state:
        x3, new_c_conv = _conformer(x2.reshape(nb, seqlen, D_MODEL), c_conv, w["conf"],
                                    n_unit=CONF_UNITS_SAMPLE, tc=seqlen)
    else:
        x3, new_c_conv = _conformer(x2.reshape(nb, seqlen, D_MODEL), None, w["conf"], n_unit=1, tc=CONF_TIME_TILE)
    y = _ff(x3.reshape(nb * seqlen, D_MODEL), None, w["norm_ff1"], w["w_ff1"], w["w_ff2"], 1, w["norm_final"])
    return (y.reshape(nb, seqlen, D_MODEL),
            new_a_conv[None],
            new_a_ssm.reshape(1, nb, A_HEADS, A_HEAD_DIM, A_STATE),
            new_b_re.reshape(1, nb, B_GROUPS, B_STATE),
            new_b_im.reshape(1, nb, B_GROUPS, B_STATE),
            jnp.swapaxes(new_c_conv, 0, 1)[None])


def _prepare_weights(norm_mix, norm_ff, norm_final, w_in_ab, a_conv_w, a_conv_b, a_dt_bias, a_log, a_d, a_norm,
                     s5_lam_re, s5_lam_im, s5_log_step, s5_b_re, s5_b_im, s5_c_re, s5_c_im, s5_d, s5_w_glu,
                     s5_b_glu, w_out_ab, c_w_pw1, c_b_pw1, c_w_dw, c_b_dw, c_ln_w, c_ln_b, c_w_pw2, c_b_pw2,
                     w_ff1, w_ff2):
    row = lambda v: v.reshape(1, -1)
    a_proj = A_D_INNER + A_CONV_DIM + A_HEADS
    w_in = w_in_ab[0]
    dt_pad = LANES - A_HEADS
    w_in_r = jnp.concatenate([w_in[:, :A_D_INNER + A_CONV_DIM], w_in[:, a_proj:],
                              w_in[:, A_D_INNER + A_CONV_DIM:a_proj],
                              jnp.zeros((D_MODEL, dt_pad), F32)], axis=1).astype(BF16)
    pad_heads = lambda v: jnp.pad(v.reshape(1, A_HEADS), ((0, 0), (0, dt_pad)))
    head_lane = jnp.arange(A_D_INNER) // A_HEAD_DIM
    expand_heads = (jnp.arange(LANES)[:, None] == head_lane[None, :]).astype(BF16)
    a_conv = (a_conv_w[0], row(a_conv_b[0]))
    mamba = (pad_heads(a_dt_bias[0]), pad_heads(a_log[0]),
             row(jnp.repeat(a_d[0], A_HEAD_DIM)), row(a_norm[0]), expand_heads, expand_heads.T)

    ab_re, ab_im, bb_re, bb_im = _s5_params(s5_lam_re[0], s5_lam_im[0], s5_log_step[0], s5_b_re[0], s5_b_im[0])
    bb = jnp.concatenate([_block_diag(bb_re, S5_BLK), _block_diag(bb_im, S5_BLK)], axis=2).astype(BF16)
    c_re_t = jnp.swapaxes(s5_c_re[0], 1, 2)
    c_im_t = jnp.swapaxes(s5_c_im[0], 1, 2)
    cc = jnp.concatenate([_block_diag(c_re_t, S5_BLK), _block_diag(-c_im_t, S5_BLK)], axis=1).astype(BF16)
    s5 = (ab_re.reshape(S5_SLABS, LANES), ab_im.reshape(S5_SLABS, LANES), bb, cc, row(s5_d[0]),
          s5_w_glu[0].astype(BF16), row(s5_b_glu[0]))

    slab_major = lambda v: jnp.swapaxes(v.reshape(-1, N_SLAB, LANES), 0, 1)
    conf = (row(norm_mix[1]), c_w_pw1[0].astype(BF16), row(c_b_pw1[0]), slab_major(c_w_dw[0]), slab_major(c_b_dw[0]),
            row(c_ln_w[0]), row(c_ln_b[0]), c_w_pw2[0].astype(BF16), row(c_b_pw2[0]))
    return dict(norm_mix0=row(norm_mix[0]), w_in=w_in_r, a_conv=a_conv, mamba=mamba, s5=s5, w_out=w_out_ab[0].astype(BF16),
                norm_ff0=row(norm_ff[0]), norm_ff1=row(norm_ff[1]), w_ff1=w_ff1.astype(BF16), w_ff2=w_ff2.astype(BF16),
                conf=conf, norm_final=row(norm_final))


def kernel(x_prompt, x_sample, state_a_conv, state_a_ssm, state_b_re, state_b_im, state_c_conv, norm_mix, norm_ff, norm_final, w_in_ab, a_conv_w, a_conv_b, a_dt_bias, a_log, a_d, a_norm, s5_lam_re, s5_lam_im, s5_log_step, s5_b_re, s5_b_im, s5_c_re, s5_c_im, s5_d, s5_w_glu, s5_b_glu, w_out_ab, c_w_pw1, c_b_pw1, c_w_dw, c_b_dw, c_ln_w, c_ln_b, c_w_pw2, c_b_pw2, w_ff1, w_ff2):
    w = _prepare_weights(norm_mix, norm_ff, norm_final, w_in_ab, a_conv_w, a_conv_b, a_dt_bias, a_log, a_d, a_norm,
                         s5_lam_re, s5_lam_im, s5_log_step, s5_b_re, s5_b_im, s5_c_re, s5_c_im, s5_d, s5_w_glu,
                         s5_b_glu, w_out_ab, c_w_pw1, c_b_pw1, c_w_dw, c_b_dw, c_ln_w, c_ln_b, c_w_pw2, c_b_pw2,
                         w_ff1, w_ff2)
    prompt = _trunk(x_prompt, None, w)
    sample = _trunk(x_sample, (state_a_conv[0], state_a_ssm[0], state_b_re[0], state_b_im[0],
                               jnp.swapaxes(state_c_conv[0], 0, 1)), w)
    return (prompt[0], sample[0]) + prompt[1:] + sample[1:]
```
